```python
import math
import jax, jax.numpy as jnp
from jax import lax
import numpy as np

D_MODEL = 2048
BATCH = 1
SEQ = 8192
DEPTH = 2

GRID_W = 64
CTX_LEN = 256
Q_BLOCK = 128
ROPE_THETA = 10000.0
NORM_EPS = 1e-6

N_HEADS_A = 8
N_KV_HEADS_A = 2
HEAD_DIM_A = 128
N_HEADS_B = 4
HEAD_DIM_B = 64
VALUE_DIM_B = 2 * HEAD_DIM_B
N_FOURIER_GROUPS = 4
FOURIER_GROUP_DIM = 128

WIDTH_A = N_HEADS_A * HEAD_DIM_A
WIDTH_B = N_HEADS_B * VALUE_DIM_B
WIDTH_C = N_FOURIER_GROUPS * FOURIER_GROUP_DIM
MIX_WIDTH = WIDTH_A + WIDTH_B + WIDTH_C

QA_COLS = N_HEADS_A * HEAD_DIM_A
QB_COLS = N_HEADS_B * 2 * HEAD_DIM_B
C_COLS = WIDTH_C
KA_COLS = N_KV_HEADS_A * HEAD_DIM_A
VA_COLS = N_KV_HEADS_A * HEAD_DIM_A
KB_COLS = N_HEADS_B * 2 * HEAD_DIM_B
VB_COLS = N_HEADS_B * VALUE_DIM_B
KV_START = QA_COLS + QB_COLS + C_COLS
IN_COLS = KV_START + KA_COLS + VA_COLS + KB_COLS + VB_COLS

D_FF = 7168
N_EXPERTS = 8
TOP_K = 2
D_FF_EXPERT = 7168
N_DENSE = (DEPTH + 1) // 2
N_MOE = DEPTH // 2
N_MOD = 6

kernel_name = 'hybrid_dit_gqa_diffattn_fourier_moe'


def _rmsnorm(x, g):
    xf = x.astype(jnp.float32)
    y = xf * lax.rsqrt(jnp.mean(xf * xf, axis=-1, keepdims=True) + NORM_EPS)
    return (y * g.astype(jnp.float32)).astype(x.dtype)


def _modulate(h, shift, scale):
    return h * (1 + scale) + shift


def _axial_rope_tables(n_rows, head_dim):
    rows = jnp.repeat(jnp.arange(n_rows), GRID_W).astype(jnp.float32)
    cols = jnp.tile(jnp.arange(GRID_W), n_rows).astype(jnp.float32)
    axis_dim = head_dim // 2
    freqs = ROPE_THETA ** (-jnp.arange(0, axis_dim, 2, dtype=jnp.float32) / axis_dim)
    row_ang = rows[:, None] * freqs
    col_ang = cols[:, None] * freqs
    return (jnp.cos(row_ang), jnp.sin(row_ang), jnp.cos(col_ang), jnp.sin(col_ang))


def _rotate(x, cos, sin):
    x1, x2 = jnp.split(x, 2, axis=-1)
    return jnp.concatenate([x1 * cos - x2 * sin, x2 * cos + x1 * sin], axis=-1)


def _apply_axial_rope(x, tables):
    bshape = (1, x.shape[1]) + (1,) * (x.ndim - 3) + (tables[0].shape[-1],)
    rc, rs, cc, cs = (t.reshape(bshape) for t in tables)
    x_row, x_col = jnp.split(x.astype(jnp.float32), 2, axis=-1)
    out = jnp.concatenate([_rotate(x_row, rc, rs), _rotate(x_col, cc, cs)], axis=-1)
    return out.astype(x.dtype)


def _sweep_query_blocks(fn, q):
    b, n = q.shape[:2]
    nb = n // Q_BLOCK
    qb = jnp.swapaxes(q.reshape((b, nb, Q_BLOCK) + q.shape[2:]), 0, 1)
    out = lax.map(fn, qb)
    return jnp.swapaxes(out, 0, 1).reshape((b, n) + out.shape[3:])


def _gqa_attention(q, k, v):
    b, n = q.shape[:2]
    q = q.reshape(b, n, N_KV_HEADS_A, N_HEADS_A // N_KV_HEADS_A, HEAD_DIM_A)
    scale = HEAD_DIM_A ** -0.5

    def block(qb):
        s = jnp.einsum('bqhgd,bkhd->bhgqk', qb, k, preferred_element_type=jnp.float32) * scale
        p = jax.nn.softmax(s, axis=-1).astype(v.dtype)
        return jnp.einsum('bhgqk,bkhd->bqhgd', p, v)

    return _sweep_query_blocks(block, q).reshape(b, n, WIDTH_A)


def _diff_attention(q, k, v, lam, g_subln, lambda_init):
    b, n = q.shape[:2]
    scale = HEAD_DIM_B ** -0.5
    k1, k2 = k[..., 0, :], k[..., 1, :]

    def block(qb):
        s1 = jnp.einsum('bqhd,bkhd->bhqk', qb[..., 0, :], k1, preferred_element_type=jnp.float32) * scale
        s2 = jnp.einsum('bqhd,bkhd->bhqk', qb[..., 1, :], k2, preferred_element_type=jnp.float32) * scale
        p = jax.nn.softmax(s1, axis=-1) - lam * jax.nn.softmax(s2, axis=-1)
        return jnp.einsum('bhqk,bkhe->bqhe', p.astype(v.dtype), v)

    out = _sweep_query_blocks(block, q)
    out = _rmsnorm(out, g_subln) * (1.0 - lambda_init)
    return out.reshape(b, n, WIDTH_B)


def _fourier_mix(u, w_fourier):
    b, n = u.shape[:2]
    ug = u.reshape(b, n, N_FOURIER_GROUPS, FOURIER_GROUP_DIM).astype(jnp.float32)
    f = jnp.fft.fftn(ug, axes=(1, 3), norm='ortho').real.astype(u.dtype)
    return jnp.einsum('bngc,gcd->bngd', f, w_fourier).reshape(b, n, WIDTH_C)


def _split_queries(p_q, g_q_a):
    b, n = p_q.shape[:2]
    qa = _rmsnorm(p_q[..., :QA_COLS].reshape(b, n, N_HEADS_A, HEAD_DIM_A), g_q_a)
    qb = p_q[..., QA_COLS:QA_COLS + QB_COLS].reshape(b, n, N_HEADS_B, 2, HEAD_DIM_B)
    u = p_q[..., QA_COLS + QB_COLS:]
    return qa, qb, u


def _split_keys_values(p_kv, g_k_a):
    b, m = p_kv.shape[:2]
    o1 = KA_COLS
    o2 = o1 + VA_COLS
    o3 = o2 + KB_COLS
    ka = _rmsnorm(p_kv[..., :o1].reshape(b, m, N_KV_HEADS_A, HEAD_DIM_A), g_k_a)
    va = p_kv[..., o1:o2].reshape(b, m, N_KV_HEADS_A, HEAD_DIM_A)
    kb = p_kv[..., o2:o3].reshape(b, m, N_HEADS_B, 2, HEAD_DIM_B)
    vb = p_kv[..., o3:].reshape(b, m, N_HEADS_B, VALUE_DIM_B)
    return ka, va, kb, vb


def _mix_head_groups(qa, qb, u, ka, va, kb, vb, lam, g_subln, lambda_init, w_fourier, w_out):
    y = jnp.concatenate([
        _gqa_attention(qa, ka, va),
        _diff_attention(qb, kb, vb, lam, g_subln, lambda_init),
        _fourier_mix(u, w_fourier),
    ], axis=-1)
    return y @ w_out


def _token_mixer(h_lat, h_ctx, w_in, g_q_a, g_k_a, lam, g_subln, lambda_init,
                 w_fourier, w_out, rope_a, rope_b, with_ctx_out):
    if with_ctx_out:
        p_ctx = h_ctx @ w_in
        kv_ctx = p_ctx[..., KV_START:]
    else:
        kv_ctx = h_ctx @ w_in[:, KV_START:]
    ka_c, va_c, kb_c, vb_c = _split_keys_values(kv_ctx, g_k_a)

    p_lat = h_lat @ w_in
    qa_l, qb_l, u_l = _split_queries(p_lat[..., :KV_START], g_q_a)
    ka_l, va_l, kb_l, vb_l = _split_keys_values(p_lat[..., KV_START:], g_k_a)
    qa_l = _apply_axial_rope(qa_l, rope_a)
    ka_l = _apply_axial_rope(ka_l, rope_a)
    qb_l = _apply_axial_rope(qb_l, rope_b)
    kb_l = _apply_axial_rope(kb_l, rope_b)

    def cat(a, b):
        return jnp.concatenate([a, b], axis=1)

    y_lat = _mix_head_groups(qa_l, qb_l, u_l, cat(ka_l, ka_c), cat(va_l, va_c),
                             cat(kb_l, kb_c), cat(vb_l, vb_c), lam, g_subln,
                             lambda_init, w_fourier, w_out)
    y_ctx = None
    if with_ctx_out:
        qa_c, qb_c, u_c = _split_queries(p_ctx[..., :KV_START], g_q_a)
        y_ctx = _mix_head_groups(qa_c, qb_c, u_c, ka_c, va_c, kb_c, vb_c, lam, g_subln,
                                 lambda_init, w_fourier, w_out)
    return y_lat, y_ctx


def _swiglu(h, w_gate, w_up, w_down):
    return (jax.nn.silu(h @ w_gate) * (h @ w_up)) @ w_down


def _moe_swiglu(h, w_router, w_gate, w_up, w_down):
    logits = jnp.einsum('bnd,de->bne', h, w_router, preferred_element_type=jnp.float32)
    top_logits, top_idx = lax.top_k(logits, TOP_K)
    top_w = jax.nn.softmax(top_logits, axis=-1)
    combine = jnp.sum(jax.nn.one_hot(top_idx, N_EXPERTS, dtype=jnp.float32) * top_w[..., None],
                      axis=-2).astype(h.dtype)
    out = jnp.zeros_like(h)
    for e in range(N_EXPERTS):
        out = out + combine[..., e:e + 1] * _swiglu(h, w_gate[e], w_up[e], w_down[e])
    return out


def _channel_mixer(h, layer, w_gate_dense, w_up_dense, w_down_dense,
                   w_router, w_gate_moe, w_up_moe, w_down_moe):
    j = layer // 2
    if layer % 2 == 0:
        return _swiglu(h, w_gate_dense[j], w_up_dense[j], w_down_dense[j])
    return _moe_swiglu(h, w_router[j], w_gate_moe[j], w_up_moe[j], w_down_moe[j])


def setup_inputs(seed: int = 0) -> dict:
    key = jax.random.key(seed)
    ks = iter(jax.random.split(key, 32))

    def normal(shape, scale):
        return jax.random.normal(next(ks), shape, jnp.float32) * scale

    def gain(shape):
        return 1.0 + 0.05 * jax.random.normal(next(ks), shape, jnp.float32)

    d = D_MODEL
    return {
        'x': normal((BATCH, SEQ, d), 1.0),
        'c': normal((BATCH, d), 1.0),
        'ctx': normal((BATCH, CTX_LEN, d), 1.0),
        'c_ctx': normal((d,), 1.0),
        'w_mod': normal((DEPTH, d, N_MOD * d), 0.5 * d ** -0.5),
        'b_mod': normal((DEPTH, N_MOD * d), 0.01),
        'g_pre_mix': gain((DEPTH, d)),
        'g_post_mix': gain((DEPTH, d)),
        'g_pre_ffn': gain((DEPTH, d)),
        'g_post_ffn': gain((DEPTH, d)),
        'w_in': normal((DEPTH, d, IN_COLS), d ** -0.5),
        'g_q_a': gain((DEPTH, HEAD_DIM_A)),
        'g_k_a': gain((DEPTH, HEAD_DIM_A)),
        'lambda_q1': normal((DEPTH, HEAD_DIM_B), 0.1),
        'lambda_k1': normal((DEPTH, HEAD_DIM_B), 0.1),
        'lambda_q2': normal((DEPTH, HEAD_DIM_B), 0.1),
        'lambda_k2': normal((DEPTH, HEAD_DIM_B), 0.1),
        'g_subln_b': gain((DEPTH, VALUE_DIM_B)),
        'w_fourier': normal((DEPTH, N_FOURIER_GROUPS, FOURIER_GROUP_DIM, FOURIER_GROUP_DIM),
                            FOURIER_GROUP_DIM ** -0.5),
        'w_out': normal((DEPTH, MIX_WIDTH, d), MIX_WIDTH ** -0.5),
        'w_gate_dense': normal((N_DENSE, d, D_FF), d ** -0.5),
        'w_up_dense': normal((N_DENSE, d, D_FF), d ** -0.5),
        'w_down_dense': normal((N_DENSE, D_FF, d), D_FF ** -0.5),
        'w_router': normal((N_MOE, d, N_EXPERTS), d ** -0.5),
        'w_gate_moe': normal((N_MOE, N_EXPERTS, d, D_FF_EXPERT), d ** -0.5),
        'w_up_moe': normal((N_MOE, N_EXPERTS, d, D_FF_EXPERT), d ** -0.5),
        'w_down_moe': normal((N_MOE, N_EXPERTS, D_FF_EXPERT, d), D_FF_EXPERT ** -0.5),
    }


def reference(x, c, ctx, c_ctx, w_mod, b_mod, g_pre_mix, g_post_mix, g_pre_ffn, g_post_ffn,
              w_in, g_q_a, g_k_a, lambda_q1, lambda_k1, lambda_q2, lambda_k2, g_subln_b,
              w_fourier, w_out, w_gate_dense, w_up_dense, w_down_dense, w_router,
              w_gate_moe, w_up_moe, w_down_moe):
    n_lat = x.shape[1]
    ROWS = n_lat // GRID_W
    rope_a = _axial_rope_tables(ROWS, HEAD_DIM_A)
    rope_b = _axial_rope_tables(ROWS, HEAD_DIM_B)
    cond_lat = jax.nn.silu(c)[:, None, :]
    cond_ctx = jax.nn.silu(c_ctx)[None, None, :]
    x_lat, x_ctx = x, ctx
    for layer in range(DEPTH):
        last = layer == DEPTH - 1
        sh_m, sc_m, gt_m, sh_f, sc_f, gt_f = jnp.split(cond_lat @ w_mod[layer] + b_mod[layer], N_MOD, axis=-1)
        csh_m, csc_m, cgt_m, csh_f, csc_f, cgt_f = jnp.split(cond_ctx @ w_mod[layer] + b_mod[layer], N_MOD, axis=-1)
        lambda_init = 0.8 - 0.6 * math.exp(-0.3 * layer)
        lam = (jnp.exp(jnp.sum(lambda_q1[layer].astype(jnp.float32) * lambda_k1[layer].astype(jnp.float32)))
               - jnp.exp(jnp.sum(lambda_q2[layer].astype(jnp.float32) * lambda_k2[layer].astype(jnp.float32)))
               + lambda_init)

        h_lat = _modulate(_rmsnorm(x_lat, g_pre_mix[layer]), sh_m, sc_m)
        h_ctx = _modulate(_rmsnorm(x_ctx, g_pre_mix[layer]), csh_m, csc_m)
        y_lat, y_ctx = _token_mixer(h_lat, h_ctx, w_in[layer], g_q_a[layer], g_k_a[layer], lam,
                                    g_subln_b[layer], lambda_init, w_fourier[layer], w_out[layer],
                                    rope_a, rope_b, not last)
        x_lat = x_lat + gt_m * _rmsnorm(y_lat, g_post_mix[layer])
        if not last:
            x_ctx = x_ctx + cgt_m * _rmsnorm(y_ctx, g_post_mix[layer])

        h_lat = _modulate(_rmsnorm(x_lat, g_pre_ffn[layer]), sh_f, sc_f)
        f_lat = _channel_mixer(h_lat, layer, w_gate_dense, w_up_dense, w_down_dense,
                               w_router, w_gate_moe, w_up_moe, w_down_moe)
        x_lat = x_lat + gt_f * _rmsnorm(f_lat, g_post_ffn[layer])
        if not last:
            h_ctx = _modulate(_rmsnorm(x_ctx, g_pre_ffn[layer]), csh_f, csc_f)
            f_ctx = _channel_mixer(h_ctx, layer, w_gate_dense, w_up_dense, w_down_dense,
                                   w_router, w_gate_moe, w_up_moe, w_down_moe)
            x_ctx = x_ctx + cgt_f * _rmsnorm(f_ctx, g_post_ffn[layer])
    return x_lat
```

```python
import functools
import math

import numpy as np
import jax
import jax.numpy as jnp
from jax import lax
from jax.experimental import pallas as pl
from jax.experimental.pallas import tpu as pltpu

F32 = jnp.float32
BF16 = jnp.bfloat16

GRID_W = 64
ROPE_THETA = 10000.0
NORM_EPS = 1e-6
N_HEADS_A, N_KV_HEADS_A, HEAD_DIM_A = 8, 2, 128
N_HEADS_B, HEAD_DIM_B = 4, 64
VALUE_DIM_B = 2 * HEAD_DIM_B
N_FOURIER_GROUPS, FOURIER_GROUP_DIM = 4, 128
QA_COLS = N_HEADS_A * HEAD_DIM_A
QB_COLS = N_HEADS_B * 2 * HEAD_DIM_B
C_COLS = N_FOURIER_GROUPS * FOURIER_GROUP_DIM
KA_COLS = N_KV_HEADS_A * HEAD_DIM_A
VA_COLS = KA_COLS
KB_COLS = QB_COLS
VB_COLS = N_HEADS_B * VALUE_DIM_B
KV_START = QA_COLS + QB_COLS + C_COLS
IN_COLS = KV_START + KA_COLS + VA_COLS + KB_COLS + VB_COLS
N_EXPERTS, TOP_K = 8, 2
N_MOD = 6

LANES = 128
V7X_VMEM_BYTES = 64 * 1024 * 1024
VMEM_LIMIT = 52 * 1024 * 1024

ROW_TILE = 256
NEG_BIG = -1e30


def _params(*sem):
    return pltpu.CompilerParams(dimension_semantics=sem, vmem_limit_bytes=VMEM_LIMIT)


def _silu(x):
    return x / (1.0 + jnp.exp(-x))


def _largest_tile(n, candidates):
    for c in candidates:
        if n % c == 0:
            return c
    raise ValueError(f"no tile in {candidates} divides {n}")


def _mod_kernel(c_ref, w_ref, b_ref, o_ref):
    cond = _silu(c_ref[...])
    o_ref[0] = jnp.dot(cond.astype(BF16), w_ref[0].astype(BF16),
                       preferred_element_type=F32) + b_ref[0]


def _modulation(cond_rows, w_mod, b_mod):
    depth, d, n6 = w_mod.shape
    tn = _largest_tile(n6, (1024, 512, 256, 128))
    return pl.pallas_call(
        _mod_kernel,
        grid=(depth, n6 // tn),
        in_specs=[pl.BlockSpec((8, d), lambda l, j: (0, 0)),
                  pl.BlockSpec((1, d, tn), lambda l, j: (l, 0, j)),
                  pl.BlockSpec((1, 1, tn), lambda l, j: (l, 0, j))],
        out_specs=pl.BlockSpec((1, 8, tn), lambda l, j: (l, 0, j)),
        out_shape=jax.ShapeDtypeStruct((depth, 8, n6), F32),
        compiler_params=_params("arbitrary", "arbitrary"),
        name="modulation",
    )(cond_rows, w_mod, b_mod.reshape(depth, 1, n6))


def _norm_modulate(x, g, mod, shift_row):
    ms = jnp.mean(x * x, axis=-1, keepdims=True)
    y = x * lax.rsqrt(ms + NORM_EPS) * g
    return y * (1.0 + mod[shift_row + 1:shift_row + 2]) + mod[shift_row:shift_row + 1]


def _prenorm_kernel(x_ref, g_ref, mod_ref, o_ref, *, shift_row):
    h = _norm_modulate(x_ref[...], g_ref[...], mod_ref[0], shift_row)
    o_ref[...] = h.astype(o_ref.dtype)


def _mod_spec(d, n_lat_tiles):
    return pl.BlockSpec((1, N_MOD, d), lambda i: ((i >= n_lat_tiles).astype(jnp.int32), 0, 0))


def _prenorm(x, g, mod, shift_row, n_rows, n_lat):
    d = x.shape[1]
    return pl.pallas_call(
        functools.partial(_prenorm_kernel, shift_row=shift_row),
        grid=(n_rows // ROW_TILE,),
        in_specs=[pl.BlockSpec((ROW_TILE, d), lambda i: (i, 0)),
                  pl.BlockSpec((1, d), lambda i: (0, 0)),
                  _mod_spec(d, n_lat // ROW_TILE)],
        out_specs=pl.BlockSpec((ROW_TILE, d), lambda i: (i, 0)),
        out_shape=jax.ShapeDtypeStruct((n_rows, d), BF16),
        compiler_params=_params("arbitrary"),
        name="prenorm",
    )(x, g.reshape(1, d), mod)


def _unpack_bf16_pairs(words):
    lo = lax.bitcast_convert_type(words << 16, F32)
    hi = lax.bitcast_convert_type(words & jnp.uint32(0xFFFF0000), F32)
    return jnp.concatenate([lo, hi], axis=-1).astype(BF16)


def _pack_bf16_pairs(h):
    half = h.shape[-1] // 2
    bits = lax.bitcast_convert_type(h.astype(BF16).astype(F32), jnp.uint32)
    return (bits[:, :half] >> 16) | (bits[:, half:] & jnp.uint32(0xFFFF0000))


def _gmm_kernel(tg_ref, nv_ref, lhs_ref, *rest, n_w, swiglu, packed):
    w_refs, o_ref, wb_ref = rest[:n_w], rest[n_w], rest[n_w + 1]
    m = pl.program_id(1)
    first = jnp.logical_or(m == 0, tg_ref[m] != tg_ref[jnp.maximum(m - 1, 0)])

    @pl.when(first)
    def _():
        for i in range(n_w):
            wb_ref[i] = w_refs[i][0].astype(BF16)

    @pl.when(m < nv_ref[0])
    def _():
        x = _unpack_bf16_pairs(lhs_ref[...]) if packed else lhs_ref[...]
        if swiglu:
            gate = jnp.dot(x, wb_ref[0], preferred_element_type=F32)
            up = jnp.dot(x, wb_ref[1], preferred_element_type=F32)
            out = _silu(gate) * up
        else:
            out = jnp.dot(x, wb_ref[0], preferred_element_type=F32)
        o_ref[...] = out.astype(o_ref.dtype)

    @pl.when(m >= nv_ref[0])
    def _():
        o_ref[...] = jnp.zeros(o_ref.shape, o_ref.dtype)


def _gmm(lhs, weights, tile_group, n_valid, *, tm, tn, out_dtype, swiglu=False, packed=False,
         n_rows=None, name="gmm"):
    n_rows = lhs.shape[0] if n_rows is None else n_rows
    _, k, nw = weights[0].shape
    n_w = len(weights)
    lhs_cols = lhs.shape[1]
    grid_spec = pltpu.PrefetchScalarGridSpec(
        num_scalar_prefetch=2,
        grid=(nw // tn, n_rows // tm),
        in_specs=[pl.BlockSpec((tm, lhs_cols), lambda n, m, tg, nv: (m, 0))]
        + [pl.BlockSpec((1, k, tn), lambda n, m, tg, nv: (tg[m], 0, n)) for _ in range(n_w)],
        out_specs=pl.BlockSpec((tm, tn), lambda n, m, tg, nv: (m, n)),
        scratch_shapes=[pltpu.VMEM((n_w, k, tn), BF16)],
    )
    return pl.pallas_call(
        functools.partial(_gmm_kernel, n_w=n_w, swiglu=swiglu, packed=packed),
        grid_spec=grid_spec,
        out_shape=jax.ShapeDtypeStruct((n_rows, nw), out_dtype),
        compiler_params=_params("arbitrary", "arbitrary"),
        name=name,
    )(tile_group, n_valid, lhs, *weights)


def _dense_matmul(lhs, w, *, tn, out_dtype, swiglu=False, w2=None, n_rows=None, name="matmul"):
    n_rows = lhs.shape[0] if n_rows is None else n_rows
    tm = _largest_tile(n_rows, (1024, 768, 512, 256))
    n_tiles = n_rows // tm
    weights = [w[None]] if w2 is None else [w[None], w2[None]]
    return _gmm(lhs, weights, jnp.zeros((n_tiles,), jnp.int32), jnp.full((1,), n_tiles, jnp.int32),
                tm=tm, tn=tn, out_dtype=out_dtype, swiglu=swiglu, n_rows=n_rows, name=name)


def _rope_tables(n_lat, n_ctx, head_dim, reps):
    n_rows_grid = n_lat // GRID_W
    rows = jnp.repeat(jnp.arange(n_rows_grid), GRID_W).astype(F32)
    cols = jnp.tile(jnp.arange(GRID_W), n_rows_grid).astype(F32)
    axis_dim = head_dim // 2
    freqs = ROPE_THETA ** (-jnp.arange(0, axis_dim, 2, dtype=F32) / axis_dim)
    ra, ca = rows[:, None] * freqs, cols[:, None] * freqs
    rc, rs, cc, cs = jnp.cos(ra), jnp.sin(ra), jnp.cos(ca), jnp.sin(ca)
    z = jnp.zeros_like(rs)
    cos = jnp.concatenate([rc, rc, cc, cc], axis=-1)
    sin_up = jnp.concatenate([-rs, z, -cs, z], axis=-1)
    sin_dn = jnp.concatenate([z, rs, z, cs], axis=-1)
    pad = lambda t, v: jnp.concatenate(
        [jnp.tile(t, (1, reps)), jnp.full((n_ctx, LANES), v, F32)], axis=0)
    return pad(cos, 1.0), pad(sin_up, 0.0), pad(sin_dn, 0.0)


def _rope(x, cos, sin_up, sin_dn, quarter):
    return (x * cos + pltpu.roll(x, LANES - quarter, 1) * sin_up
            + pltpu.roll(x, quarter, 1) * sin_dn)


def _head_rmsnorm(x, g):
    return x * lax.rsqrt(jnp.mean(x * x, axis=-1, keepdims=True) + NORM_EPS) * g


def _qkpost_kernel(p_ref, gq_ref, gk_ref, ca_ref, ua_ref, da_ref, cb_ref, ub_ref, db_ref,
                   qv_ref, k_ref, u_ref):
    def col(c):
        return p_ref[:, c * LANES:(c + 1) * LANES]

    rope_a = lambda x: _rope(x, ca_ref[...], ua_ref[...], da_ref[...], HEAD_DIM_A // 4)
    rope_b = lambda x: _rope(x, cb_ref[...], ub_ref[...], db_ref[...], HEAD_DIM_B // 4)
    scale_a, scale_b = HEAD_DIM_A ** -0.5, HEAD_DIM_B ** -0.5
    c0 = 0
    out = 0
    for h in range(N_HEADS_A):
        x = rope_a(_head_rmsnorm(col(c0 + h), gq_ref[...])) * scale_a
        qv_ref[:, (out + h) * LANES:(out + h + 1) * LANES] = x.astype(BF16)
    c0 += N_HEADS_A
    out += N_HEADS_A
    for h in range(N_HEADS_B):
        x = rope_b(col(c0 + h)) * scale_b
        qv_ref[:, (out + h) * LANES:(out + h + 1) * LANES] = x.astype(BF16)
    c0 += N_HEADS_B
    out += N_HEADS_B
    for g in range(N_FOURIER_GROUPS):
        u_ref[:, g * LANES:(g + 1) * LANES] = col(c0 + g).astype(BF16)
    c0 += N_FOURIER_GROUPS
    for h in range(N_KV_HEADS_A):
        x = rope_a(_head_rmsnorm(col(c0 + h), gk_ref[...]))
        k_ref[:, h * LANES:(h + 1) * LANES] = x.astype(BF16)
    c0 += N_KV_HEADS_A
    for h in range(N_KV_HEADS_A):
        qv_ref[:, (out + h) * LANES:(out + h + 1) * LANES] = col(c0 + h).astype(BF16)
    c0 += N_KV_HEADS_A
    out += N_KV_HEADS_A
    for h in range(N_HEADS_B):
        x = rope_b(col(c0 + h))
        k_ref[:, (N_KV_HEADS_A + h) * LANES:(N_KV_HEADS_A + h + 1) * LANES] = x.astype(BF16)
    c0 += N_HEADS_B
    for h in range(N_HEADS_B):
        qv_ref[:, (out + h) * LANES:(out + h + 1) * LANES] = col(c0 + h).astype(BF16)


QV_COLS = QA_COLS + QB_COLS + VA_COLS + VB_COLS
K_COLS = KA_COLS + KB_COLS
QB_BLK = QA_COLS // LANES
VA_BLK = (QA_COLS + QB_COLS) // LANES
VB_BLK = VA_BLK + VA_COLS // LANES


def _qkpost(p, g_q, g_k, rope_a, rope_b):
    t = p.shape[0]
    row = lambda w: pl.BlockSpec((ROW_TILE, w), lambda i: (i, 0))
    vec = pl.BlockSpec((1, LANES), lambda i: (0, 0))
    return pl.pallas_call(
        _qkpost_kernel,
        grid=(t // ROW_TILE,),
        in_specs=[row(IN_COLS), vec, vec] + [row(LANES)] * 6,
        out_specs=[row(QV_COLS), row(K_COLS), row(C_COLS)],
        out_shape=[jax.ShapeDtypeStruct((t, QV_COLS), BF16),
                   jax.ShapeDtypeStruct((t, K_COLS), BF16),
                   jax.ShapeDtypeStruct((t, C_COLS), BF16)],
        compiler_params=_params("arbitrary"),
        name="qk_post",
    )(p, g_q.reshape(1, LANES), g_k.reshape(1, LANES), *rope_a, *rope_b)


ATTN_TQ = 256
ATTN_TK = 512


def _flash(q_stack, kt_lat_ref, kt_ctx_ref, v_ref, m_ref, l_ref, acc_ref, *, n_lat, n_ctx, is_lat):
    m_ref[...] = jnp.full(m_ref.shape, NEG_BIG, F32)
    l_ref[...] = jnp.zeros(l_ref.shape, F32)
    acc_ref[...] = jnp.zeros(acc_ref.shape, F32)

    def chunk(kt, v):
        s = jnp.dot(q_stack, kt, preferred_element_type=F32)
        m_old = m_ref[...]
        m_new = jnp.maximum(m_old, jnp.max(s, axis=-1, keepdims=True))
        alpha = jnp.exp(m_old - m_new)
        p = jnp.exp(s - m_new)
        l_ref[...] = alpha * l_ref[...] + jnp.sum(p, axis=-1, keepdims=True)
        acc_ref[...] = alpha * acc_ref[...] + jnp.dot(p.astype(BF16), v, preferred_element_type=F32)
        m_ref[...] = m_new

    @pl.when(is_lat)
    def _():
        def body(j, carry):
            off = pl.multiple_of(j * ATTN_TK, ATTN_TK)
            chunk(kt_lat_ref[j], v_ref[pl.ds(off, ATTN_TK), :])
            return carry
        lax.fori_loop(0, n_lat // ATTN_TK, body, 0)

    chunk(kt_ctx_ref[...], v_ref[n_lat:n_lat + n_ctx, :])


def _gqa_kernel(q_ref, ktl_ref, ktc_ref, v_ref, o_ref, m_ref, l_ref, acc_ref, *, n_lat, n_ctx):
    group = N_HEADS_A // N_KV_HEADS_A
    q = q_ref[...]
    q_stack = jnp.concatenate([q[:, h * LANES:(h + 1) * LANES] for h in range(group)], axis=0)
    is_lat = pl.program_id(1) < n_lat // ATTN_TQ
    _flash(q_stack, ktl_ref, ktc_ref, v_ref, m_ref, l_ref, acc_ref,
           n_lat=n_lat, n_ctx=n_ctx, is_lat=is_lat)
    out = acc_ref[...] / l_ref[...]
    for h in range(group):
        o_ref[:, h * LANES:(h + 1) * LANES] = out[h * ATTN_TQ:(h + 1) * ATTN_TQ].astype(o_ref.dtype)


def _attn_scratch(rows):
    return [pltpu.VMEM((rows, 1), F32), pltpu.VMEM((rows, 1), F32), pltpu.VMEM((rows, LANES), F32)]


def _gqa_attention(qv, kt_lat, kt_ctx, n_lat, n_ctx, n_q_rows):
    t = n_lat + n_ctx
    group = N_HEADS_A // N_KV_HEADS_A
    n_chunks = n_lat // ATTN_TK
    return pl.pallas_call(
        functools.partial(_gqa_kernel, n_lat=n_lat, n_ctx=n_ctx),
        grid=(N_KV_HEADS_A, n_q_rows // ATTN_TQ),
        in_specs=[pl.BlockSpec((ATTN_TQ, group * LANES), lambda g, i: (i, g)),
                  pl.BlockSpec((n_chunks, LANES, ATTN_TK), lambda g, i: (0, g, 0)),
                  pl.BlockSpec((LANES, n_ctx), lambda g, i: (g, 0)),
                  pl.BlockSpec((t, LANES), lambda g, i: (0, VA_BLK + g))],
        out_specs=pl.BlockSpec((ATTN_TQ, group * LANES), lambda g, i: (i, g)),
        out_shape=jax.ShapeDtypeStruct((n_q_rows, QA_COLS), BF16),
        scratch_shapes=_attn_scratch(group * ATTN_TQ),
        compiler_params=_params("arbitrary", "arbitrary"),
        name="gqa_attention",
    )(qv, kt_lat, kt_ctx, qv)


def _diff_kernel(q_ref, ktl_ref, ktc_ref, v_ref, lam_ref, gs_ref, o_ref, m_ref, l_ref, acc_ref,
                 *, n_lat, n_ctx, lambda_init):
    q = q_ref[...]
    lane = lax.broadcasted_iota(jnp.int32, q.shape, 1)
    zero = jnp.zeros_like(q)
    q_stack = jnp.concatenate([jnp.where(lane < HEAD_DIM_B, q, zero),
                               jnp.where(lane >= HEAD_DIM_B, q, zero)], axis=0)
    is_lat = pl.program_id(1) < n_lat // ATTN_TQ
    _flash(q_stack, ktl_ref, ktc_ref, v_ref, m_ref, l_ref, acc_ref,
           n_lat=n_lat, n_ctx=n_ctx, is_lat=is_lat)
    lv = lam_ref[...]
    lam = (jnp.exp(jnp.sum(lv[0:1] * lv[1:2], axis=-1, keepdims=True))
           - jnp.exp(jnp.sum(lv[2:3] * lv[3:4], axis=-1, keepdims=True)) + lambda_init)
    out = acc_ref[...] / l_ref[...]
    diff = out[:ATTN_TQ] - lam * out[ATTN_TQ:]
    o_ref[...] = (_head_rmsnorm(diff, gs_ref[...]) * (1.0 - lambda_init)).astype(o_ref.dtype)


def _diff_attention(qv, kt_lat, kt_ctx, lam_vecs, g_subln, lambda_init, n_lat, n_ctx, n_q_rows):
    t = n_lat + n_ctx
    n_chunks = n_lat // ATTN_TK
    return pl.pallas_call(
        functools.partial(_diff_kernel, n_lat=n_lat, n_ctx=n_ctx, lambda_init=lambda_init),
        grid=(N_HEADS_B, n_q_rows // ATTN_TQ),
        in_specs=[pl.BlockSpec((ATTN_TQ, LANES), lambda h, i: (i, QB_BLK + h)),
                  pl.BlockSpec((n_chunks, LANES, ATTN_TK), lambda h, i: (0, N_KV_HEADS_A + h, 0)),
                  pl.BlockSpec((LANES, n_ctx), lambda h, i: (N_KV_HEADS_A + h, 0)),
                  pl.BlockSpec((t, LANES), lambda h, i: (0, VB_BLK + h)),
                  pl.BlockSpec((4, HEAD_DIM_B), lambda h, i: (0, 0)),
                  pl.BlockSpec((1, LANES), lambda h, i: (0, 0))],
        out_specs=pl.BlockSpec((ATTN_TQ, LANES), lambda h, i: (i, h)),
        out_shape=jax.ShapeDtypeStruct((n_q_rows, VB_COLS), BF16),
        scratch_shapes=_attn_scratch(2 * ATTN_TQ),
        compiler_params=_params("arbitrary", "arbitrary"),
        name="diff_attention",
    )(qv, kt_lat, kt_ctx, qv, lam_vecs, g_subln.reshape(1, LANES))


def _dft_cos_sin(n):
    j = np.arange(n, dtype=np.float64)
    ang = 2.0 * np.pi * np.outer(j, j) / n
    return np.cos(ang), np.sin(ang)


def _channel_mix_weights(cs_ref, wf_ref, ab_ref):
    cs = cs_ref[...].astype(BF16)
    for g in range(N_FOURIER_GROUPS):
        ab_ref[g] = jnp.dot(cs, wf_ref[g].astype(BF16), preferred_element_type=F32).astype(BF16)


def _channel_mix(xr, xi, ab_ref, norm):
    outs = []
    for g in range(N_FOURIER_GROUPS):
        sl = slice(g * LANES, (g + 1) * LANES)
        lhs = jnp.concatenate([xr[:, sl], xi[:, sl]], axis=-1).astype(BF16)
        outs.append(jnp.dot(lhs, ab_ref[g], preferred_element_type=F32))
    return jnp.concatenate(outs, axis=-1) * norm


def _fourier_stage1_kernel(w_ref, u_ref, o_ref):
    o_ref[...] = jnp.dot(w_ref[...].astype(BF16), u_ref[...],
                         preferred_element_type=F32).astype(o_ref.dtype)


def _fourier_stage2_kernel(a_ref, tc_ref, ts_ref, w3_ref, cs_ref, wf_ref, o_ref, ab_ref, *, norm):
    @pl.when(pl.program_id(0) == 0)
    def _():
        _channel_mix_weights(cs_ref, wf_ref, ab_ref)

    ar = a_ref[0, 0].astype(F32)
    ai = a_ref[1, 0].astype(F32)
    tc, ts = tc_ref[0], ts_ref[0]
    br = ar * tc + ai * ts
    bi = ai * tc - ar * ts
    stack = jnp.concatenate([br, bi], axis=0).astype(BF16)
    x = jnp.dot(w3_ref[...].astype(BF16), stack, preferred_element_type=F32)
    half = x.shape[0] // 2
    o_ref[...] = _channel_mix(x[:half], x[half:], ab_ref, norm).astype(o_ref.dtype)


def _fourier_ctx_kernel(u_ref, w_ref, cs_ref, wf_ref, o_ref, ab_ref, *, norm):
    _channel_mix_weights(cs_ref, wf_ref, ab_ref)
    x = jnp.dot(w_ref[...].astype(BF16), u_ref[...], preferred_element_type=F32)
    half = x.shape[0] // 2
    o_ref[...] = _channel_mix(x[:half], x[half:], ab_ref, norm).astype(o_ref.dtype)


def _fourier_latent(u_lat, w_fourier):
    n = u_lat.shape[0]
    n2 = LANES
    n1 = n // n2
    c1, s1 = _dft_cos_sin(n1)
    w1 = jnp.asarray(np.concatenate([c1, -s1], axis=0), F32)
    k1n2 = 2.0 * np.pi * np.outer(np.arange(n1), np.arange(n2)) / n
    tw_c = jnp.asarray(np.cos(k1n2)[:, :, None], F32)
    tw_s = jnp.asarray(np.sin(k1n2)[:, :, None], F32)
    c2, s2 = _dft_cos_sin(n2)
    w3 = jnp.asarray(np.block([[c2, s2], [-s2, c2]]), F32)
    cc, sc = _dft_cos_sin(FOURIER_GROUP_DIM)
    cs = jnp.asarray(np.concatenate([cc, sc], axis=0), F32)
    width = n2 * C_COLS
    cb = _largest_tile(width, (4096, 2048, 1024, 512))
    stage1 = pl.pallas_call(
        _fourier_stage1_kernel,
        grid=(width // cb,),
        in_specs=[pl.BlockSpec((2 * n1, n1), lambda j: (0, 0)),
                  pl.BlockSpec((n1, cb), lambda j: (0, j))],
        out_specs=pl.BlockSpec((2 * n1, cb), lambda j: (0, j)),
        out_shape=jax.ShapeDtypeStruct((2 * n1, width), BF16),
        compiler_params=_params("arbitrary"),
        name="fourier_stage1",
    )(w1, u_lat.reshape(n1, width))
    a = stage1.reshape(2, n1, n2, C_COLS)
    norm = 1.0 / math.sqrt(n * FOURIER_GROUP_DIM)
    out = pl.pallas_call(
        functools.partial(_fourier_stage2_kernel, norm=norm),
        grid=(n1,),
        in_specs=[pl.BlockSpec((2, 1, n2, C_COLS), lambda k: (0, k, 0, 0)),
                  pl.BlockSpec((1, n2, 1), lambda k: (k, 0, 0)),
                  pl.BlockSpec((1, n2, 1), lambda k: (k, 0, 0)),
                  pl.BlockSpec((2 * n2, 2 * n2), lambda k: (0, 0)),
                  pl.BlockSpec((2 * FOURIER_GROUP_DIM, FOURIER_GROUP_DIM), lambda k: (0, 0)),
                  pl.BlockSpec((N_FOURIER_GROUPS, FOURIER_GROUP_DIM, FOURIER_GROUP_DIM),
                               lambda k: (0, 0, 0))],
        out_specs=pl.BlockSpec((n2, C_COLS), lambda k: (0, k)),
        out_shape=jax.ShapeDtypeStruct((n2, n1 * C_COLS), BF16),
        scratch_shapes=[pltpu.VMEM((N_FOURIER_GROUPS, 2 * FOURIER_GROUP_DIM, FOURIER_GROUP_DIM), BF16)],
        compiler_params=_params("arbitrary"),
        name="fourier_stage2",
    )(a, tw_c, tw_s, w3, cs, w_fourier)
    return out.reshape(n, C_COLS)


def _fourier_ctx(u_ctx, w_fourier):
    m = u_ctx.shape[0]
    c, s = _dft_cos_sin(m)
    w = jnp.asarray(np.concatenate([c, -s], axis=0), F32)
    cc, sc = _dft_cos_sin(FOURIER_GROUP_DIM)
    cs = jnp.asarray(np.concatenate([cc, sc], axis=0), F32)
    norm = 1.0 / math.sqrt(m * FOURIER_GROUP_DIM)
    return pl.pallas_call(
        functools.partial(_fourier_ctx_kernel, norm=norm),
        out_shape=jax.ShapeDtypeStruct((m, C_COLS), BF16),
        scratch_shapes=[pltpu.VMEM((N_FOURIER_GROUPS, 2 * FOURIER_GROUP_DIM, FOURIER_GROUP_DIM), BF16)],
        compiler_params=pltpu.CompilerParams(vmem_limit_bytes=VMEM_LIMIT),
        name="fourier_ctx",
    )(u_ctx, w, cs, w_fourier)


def _gated_residual(x, f, g, mod, gate_row):
    ms = jnp.mean(f * f, axis=-1, keepdims=True)
    return x + mod[gate_row:gate_row + 1] * (f * lax.rsqrt(ms + NORM_EPS) * g)


def _residual_kernel(x_ref, f_ref, g_ref, mod_ref, o_ref, *, gate_row):
    o_ref[...] = _gated_residual(x_ref[...], f_ref[...], g_ref[...], mod_ref[0], gate_row)


def _post_residual(x, f, g, mod, gate_row, n_rows, n_lat):
    d = x.shape[1]
    row = pl.BlockSpec((ROW_TILE, d), lambda i: (i, 0))
    return pl.pallas_call(
        functools.partial(_residual_kernel, gate_row=gate_row),
        grid=(n_rows // ROW_TILE,),
        in_specs=[row, row, pl.BlockSpec((1, d), lambda i: (0, 0)), _mod_spec(d, n_lat // ROW_TILE)],
        out_specs=row,
        out_shape=jax.ShapeDtypeStruct((n_rows, d), F32),
        compiler_params=_params("arbitrary"),
        name="post_residual",
    )(x, f, g.reshape(1, d), mod)


MOE_TM = 512
R_IDX, R_W, R_RANK = 0, 2, 4


def _router_kernel(x_ref, g_ref, mod_ref, wr_ref, h_ref, route_ref, cnt_ref, carry_ref, *, shift_row):
    @pl.when(pl.program_id(0) == 0)
    def _():
        carry_ref[...] = jnp.zeros(carry_ref.shape, F32)

    h = _norm_modulate(x_ref[...], g_ref[...], mod_ref[0], shift_row)
    h_ref[...] = _pack_bf16_pairs(h)
    logits = jnp.dot(h, wr_ref[...], preferred_element_type=F32, precision=lax.Precision.HIGHEST)
    rows = logits.shape[0]
    lane = lax.broadcasted_iota(jnp.int32, logits.shape, 1).astype(F32)
    neg = jnp.full(logits.shape, -jnp.inf, F32)
    l1 = jnp.where(lane < N_EXPERTS, logits, neg)
    m1 = jnp.max(l1, axis=-1, keepdims=True)
    i1 = jnp.min(jnp.where(l1 == m1, lane, float(LANES)), axis=-1, keepdims=True)
    l2 = jnp.where(lane == i1, neg, l1)
    m2 = jnp.max(l2, axis=-1, keepdims=True)
    i2 = jnp.min(jnp.where(l2 == m2, lane, float(LANES)), axis=-1, keepdims=True)
    e2 = jnp.exp(m2 - m1)
    w1 = 1.0 / (1.0 + e2)
    w2 = e2 / (1.0 + e2)
    hit1 = lane == i1
    hit2 = lane == i2
    onehot = jnp.logical_or(hit1, hit2).astype(BF16)
    r_i = lax.broadcasted_iota(jnp.int32, (rows, rows), 0)
    c_i = lax.broadcasted_iota(jnp.int32, (rows, rows), 1)
    before = (c_i < r_i).astype(BF16)
    base = carry_ref[...] + jnp.dot(before, onehot, preferred_element_type=F32)
    rank1 = jnp.sum(jnp.where(hit1, base, 0.0), axis=-1, keepdims=True)
    rank2 = jnp.sum(jnp.where(hit2, base, 0.0), axis=-1, keepdims=True)
    carry_ref[...] = carry_ref[...] + jnp.sum(onehot.astype(F32), axis=0, keepdims=True)
    rec = jnp.zeros(logits.shape, F32)
    for off, val in ((R_IDX, i1), (R_IDX + 1, i2), (R_W, w1), (R_W + 1, w2),
                     (R_RANK, rank1), (R_RANK + 1, rank2)):
        rec = jnp.where(lane == off, val, rec)
    route_ref[...] = rec
    cnt_ref[...] = jnp.broadcast_to(carry_ref[...], cnt_ref.shape)


def _router(x, g, mod, w_router, n_lat):
    d = x.shape[1]
    wr = jnp.zeros((d, LANES), F32).at[:, :N_EXPERTS].set(w_router)
    return pl.pallas_call(
        functools.partial(_router_kernel, shift_row=3),
        grid=(n_lat // ROW_TILE,),
        in_specs=[pl.BlockSpec((ROW_TILE, d), lambda i: (i, 0)),
                  pl.BlockSpec((1, d), lambda i: (0, 0)),
                  pl.BlockSpec((1, N_MOD, d), lambda i: (0, 0, 0)),
                  pl.BlockSpec((d, LANES), lambda i: (0, 0))],
        out_specs=[pl.BlockSpec((ROW_TILE, d // 2), lambda i: (i, 0)),
                   pl.BlockSpec((ROW_TILE, LANES), lambda i: (i, 0)),
                   pl.BlockSpec((8, LANES), lambda i: (0, 0))],
        out_shape=[jax.ShapeDtypeStruct((n_lat, d // 2), jnp.uint32),
                   jax.ShapeDtypeStruct((n_lat, LANES), F32),
                   jax.ShapeDtypeStruct((8, LANES), F32)],
        scratch_shapes=[pltpu.VMEM((1, LANES), F32)],
        compiler_params=_params("arbitrary"),
        name="router",
    )(x, g.reshape(1, d), mod, wr)


def _row_copy(src_hbm, src_row, dst_ref, dst_row, sem):
    return pltpu.make_async_copy(src_hbm.at[pl.ds(src_row, 1)], dst_ref.at[pl.ds(dst_row, 1)], sem)


def _dispatch_kernel(dest_ref, h_hbm, init_hbm, o_hbm, sem):
    del init_hbm
    base = pl.program_id(0) * ROW_TILE

    def issue(r, carry):
        t = base + r
        for k in range(TOP_K):
            _row_copy(h_hbm, t, o_hbm, dest_ref[TOP_K * t + k], sem).start()
        return carry
    lax.fori_loop(0, ROW_TILE, issue, 0)

    def drain(r, carry):
        for k in range(TOP_K):
            _row_copy(h_hbm, 0, o_hbm, 0, sem).wait()
        return carry
    lax.fori_loop(0, ROW_TILE, drain, 0)


def _dispatch(dest_flat, h_packed, n_slots):
    n, half = h_packed.shape
    grid_spec = pltpu.PrefetchScalarGridSpec(
        num_scalar_prefetch=1,
        grid=(n // ROW_TILE,),
        in_specs=[pl.BlockSpec(memory_space=pl.ANY), pl.BlockSpec(memory_space=pl.ANY)],
        out_specs=pl.BlockSpec(memory_space=pl.ANY),
        scratch_shapes=[pltpu.SemaphoreType.DMA(())],
    )
    return pl.pallas_call(
        _dispatch_kernel,
        grid_spec=grid_spec,
        out_shape=jax.ShapeDtypeStruct((n_slots, half), jnp.uint32),
        input_output_aliases={2: 0},
        compiler_params=_params("arbitrary"),
        name="moe_dispatch",
    )(dest_flat, h_packed, jnp.zeros((n_slots, half), jnp.uint32))


def _combine_kernel(dest_ref, y_hbm, route_ref, x_ref, g_ref, mod_ref, o_ref, buf_ref, sem, *, gate_row):
    base = pl.program_id(0) * ROW_TILE

    def issue(r, carry):
        for k in range(TOP_K):
            _row_copy(y_hbm, dest_ref[TOP_K * (base + r) + k], buf_ref.at[k], r, sem).start()
        return carry
    lax.fori_loop(0, ROW_TILE, issue, 0)

    def drain(r, carry):
        for k in range(TOP_K):
            _row_copy(y_hbm, 0, buf_ref.at[k], 0, sem).wait()
        return carry
    lax.fori_loop(0, ROW_TILE, drain, 0)

    route = route_ref[...]
    f = route[:, R_W:R_W + 1] * buf_ref[0] + route[:, R_W + 1:R_W + 2] * buf_ref[1]
    o_ref[...] = _gated_residual(x_ref[...], f, g_ref[...], mod_ref[0], gate_row)


def _combine_residual(dest_flat, y_sorted, route, x, g, mod, gate_row, n_lat):
    d = x.shape[1]
    row = lambda w: pl.BlockSpec((ROW_TILE, w), lambda i, dest: (i, 0))
    grid_spec = pltpu.PrefetchScalarGridSpec(
        num_scalar_prefetch=1,
        grid=(n_lat // ROW_TILE,),
        in_specs=[pl.BlockSpec(memory_space=pl.ANY), row(LANES), row(d),
                  pl.BlockSpec((1, d), lambda i, dest: (0, 0)),
                  pl.BlockSpec((1, N_MOD, d), lambda i, dest: (0, 0, 0))],
        out_specs=row(d),
        scratch_shapes=[pltpu.VMEM((TOP_K, ROW_TILE, d), F32), pltpu.SemaphoreType.DMA(())],
    )
    return pl.pallas_call(
        functools.partial(_combine_kernel, gate_row=gate_row),
        grid_spec=grid_spec,
        out_shape=jax.ShapeDtypeStruct((n_lat, d), F32),
        compiler_params=_params("arbitrary"),
        name="moe_combine",
    )(dest_flat, y_sorted, route, x, g.reshape(1, d), mod)


def _moe_layer(x, g_pre, g_post, mod, w_router, w_gate, w_up, w_down, n_lat):
    d_ff = w_gate.shape[-1]
    h_packed, route, cnt = _router(x, g_pre, mod, w_router, n_lat)
    experts = route[:, R_IDX:R_IDX + TOP_K].astype(jnp.int32)
    ranks = route[:, R_RANK:R_RANK + TOP_K].astype(jnp.int32)
    counts = cnt[0, :N_EXPERTS].astype(jnp.int32)
    padded = ((counts + MOE_TM - 1) // MOE_TM) * MOE_TM
    ends = jnp.cumsum(padded)
    starts = ends - padded
    dest = (starts[experts] + ranks).reshape(-1)
    n_tiles = TOP_K * n_lat // MOE_TM + N_EXPERTS
    n_slots = n_tiles * MOE_TM
    tile_group = jnp.minimum(
        jnp.searchsorted(ends, jnp.arange(n_tiles, dtype=jnp.int32) * MOE_TM, side="right"),
        N_EXPERTS - 1).astype(jnp.int32)
    n_valid = (ends[-1:] // MOE_TM).astype(jnp.int32)

    hs = _dispatch(dest, h_packed, n_slots)
    act = _gmm(hs, [w_gate, w_up], tile_group, n_valid, tm=MOE_TM,
               tn=_largest_tile(d_ff, (512, 256, 128)), out_dtype=BF16, swiglu=True, packed=True,
               name="moe_gate_up")
    y = _gmm(act, [w_down], tile_group, n_valid, tm=MOE_TM,
             tn=_largest_tile(x.shape[1], (256, 128)), out_dtype=F32, name="moe_down")
    return _combine_residual(dest, y, route, x, g_post, mod, 5, n_lat)


def kernel(x, c, ctx, c_ctx, w_mod, b_mod, g_pre_mix, g_post_mix, g_pre_ffn, g_post_ffn, w_in, g_q_a, g_k_a, lambda_q1, lambda_k1, lambda_q2, lambda_k2, g_subln_b, w_fourier, w_out, w_gate_dense, w_up_dense, w_down_dense, w_router, w_gate_moe, w_up_moe, w_down_moe):
    batch, n_lat, d = x.shape
    n_ctx = ctx.shape[1]
    depth = w_mod.shape[0]
    assert batch == 1 and n_ctx == ROW_TILE and n_lat % max(ATTN_TK, MOE_TM, GRID_W) == 0
    assert depth == 2, "layer 0 dense, layer 1 (last) routed experts"
    t = n_lat + n_ctx

    cond_rows = jnp.concatenate([c, c_ctx[None, :], jnp.zeros((6, d), F32)], axis=0)
    mod_all = _modulation(cond_rows, w_mod, b_mod)
    rope_a = _rope_tables(n_lat, n_ctx, HEAD_DIM_A, 1)
    rope_b = _rope_tables(n_lat, n_ctx, HEAD_DIM_B, 2)

    xs = jnp.concatenate([x[0], ctx[0]], axis=0)
    for layer in range(depth):
        last = layer == depth - 1
        n_out = n_lat if last else t
        mod = mod_all[layer, :2].reshape(2, N_MOD, d)
        lambda_init = 0.8 - 0.6 * math.exp(-0.3 * layer)

        h = _prenorm(xs, g_pre_mix[layer], mod, 0, t, n_lat)
        p = _dense_matmul(h, w_in[layer], tn=512, out_dtype=F32, name="in_proj")
        qv, kk, u = _qkpost(p, g_q_a[layer], g_k_a[layer], rope_a, rope_b)
        kt_lat = kk[:n_lat].reshape(n_lat // ATTN_TK, ATTN_TK, K_COLS).transpose(0, 2, 1)
        kt_ctx = kk[n_lat:].T
        y_a = _gqa_attention(qv, kt_lat, kt_ctx, n_lat, n_ctx, n_out)
        lam_vecs = jnp.stack([lambda_q1[layer], lambda_k1[layer], lambda_q2[layer], lambda_k2[layer]])
        y_b = _diff_attention(qv, kt_lat, kt_ctx, lam_vecs, g_subln_b[layer], lambda_init,
                              n_lat, n_ctx, n_out)
        y_c = _fourier_latent(u[:n_lat], w_fourier[layer])
        if not last:
            y_c = jnp.concatenate([y_c, _fourier_ctx(u[n_lat:], w_fourier[layer])], axis=0)
        y = jnp.concatenate([y_a, y_b, y_c], axis=-1)
        f = _dense_matmul(y, w_out[layer], tn=512, out_dtype=F32, name="out_proj")
        xs = _post_residual(xs, f, g_post_mix[layer], mod, 2, n_out, n_lat)

        j = layer // 2
        if layer % 2 == 0:
            h = _prenorm(xs, g_pre_ffn[layer], mod, 3, n_out, n_lat)
            act = _dense_matmul(h, w_gate_dense[j], w2=w_up_dense[j], swiglu=True, tn=512,
                                out_dtype=BF16, name="ffn_gate_up")
            f = _dense_matmul(act, w_down_dense[j], tn=256, out_dtype=F32, name="ffn_down")
            xs = _post_residual(xs, f, g_post_ffn[layer], mod, 5, n_out, n_lat)
        else:
            xs = _moe_layer(xs, g_pre_ffn[layer], g_post_ffn[layer], mod, w_router[j],
                            w_gate_moe[j], w_up_moe[j], w_down_moe[j], n_lat)
    return xs[:n_lat][None]
```

```python
import functools
import math

import numpy as np
import jax
import jax.numpy as jnp
from jax import lax
from jax.experimental import pallas as pl
from jax.experimental.pallas import tpu as pltpu

F32 = jnp.float32
BF16 = jnp.bfloat16

GRID_W = 64
ROPE_THETA = 10000.0
NORM_EPS = 1e-6
N_HEADS_A, N_KV_HEADS_A, HEAD_DIM_A = 8, 2, 128
N_HEADS_B, HEAD_DIM_B = 4, 64
VALUE_DIM_B = 2 * HEAD_DIM_B
N_FOURIER_GROUPS, FOURIER_GROUP_DIM = 4, 128
QA_COLS = N_HEADS_A * HEAD_DIM_A
QB_COLS = N_HEADS_B * 2 * HEAD_DIM_B
C_COLS = N_FOURIER_GROUPS * FOURIER_GROUP_DIM
KA_COLS = N_KV_HEADS_A * HEAD_DIM_A
VA_COLS = KA_COLS
KB_COLS = QB_COLS
VB_COLS = N_HEADS_B * VALUE_DIM_B
KV_START = QA_COLS + QB_COLS + C_COLS
IN_COLS = KV_START + KA_COLS + VA_COLS + KB_COLS + VB_COLS
N_EXPERTS, TOP_K = 8, 2
N_MOD = 6

LANES = 128
V7X_VMEM_BYTES = 64 * 1024 * 1024
VMEM_LIMIT = 52 * 1024 * 1024

ROW_TILE = 256
NEG_BIG = -1e30
LOG2_E = math.log2(math.e)


def _params(*sem):
    return pltpu.CompilerParams(dimension_semantics=sem, vmem_limit_bytes=VMEM_LIMIT)


def _silu(x):
    return x / (1.0 + jnp.exp(-x))


def _largest_tile(n, candidates):
    for c in candidates:
        if n % c == 0:
            return c
    raise ValueError(f"no tile in {candidates} divides {n}")


def _mod_kernel(c_ref, w_ref, b_ref, o_ref):
    cond = _silu(c_ref[...])
    o_ref[0] = jnp.dot(cond.astype(BF16), w_ref[0].astype(BF16),
                       preferred_element_type=F32) + b_ref[0]


def _modulation(cond_rows, w_mod, b_mod):
    depth, d, n6 = w_mod.shape
    tn = _largest_tile(n6, (1024, 512, 256, 128))
    return pl.pallas_call(
        _mod_kernel,
        grid=(depth, n6 // tn),
        in_specs=[pl.BlockSpec((8, d), lambda l, j: (0, 0)),
                  pl.BlockSpec((1, d, tn), lambda l, j: (l, 0, j)),
                  pl.BlockSpec((1, 1, tn), lambda l, j: (l, 0, j))],
        out_specs=pl.BlockSpec((1, 8, tn), lambda l, j: (l, 0, j)),
        out_shape=jax.ShapeDtypeStruct((depth, 8, n6), F32),
        compiler_params=_params("arbitrary", "arbitrary"),
        name="modulation",
    )(cond_rows, w_mod, b_mod.reshape(depth, 1, n6))


def _norm_modulate(x, g, mod, shift_row):
    ms = jnp.mean(x * x, axis=-1, keepdims=True)
    y = x * lax.rsqrt(ms + NORM_EPS) * g
    return y * (1.0 + mod[shift_row + 1:shift_row + 2]) + mod[shift_row:shift_row + 1]


def _prenorm_kernel(x_ref, g_ref, mod_ref, o_ref, *, shift_row):
    h = _norm_modulate(x_ref[...], g_ref[...], mod_ref[0], shift_row)
    o_ref[...] = h.astype(o_ref.dtype)


def _mod_spec(d, n_lat_tiles):
    return pl.BlockSpec((1, N_MOD, d), lambda i: ((i >= n_lat_tiles).astype(jnp.int32), 0, 0))


def _prenorm(x, g, mod, shift_row, n_rows, n_lat):
    d = x.shape[1]
    return pl.pallas_call(
        functools.partial(_prenorm_kernel, shift_row=shift_row),
        grid=(n_rows // ROW_TILE,),
        in_specs=[pl.BlockSpec((ROW_TILE, d), lambda i: (i, 0)),
                  pl.BlockSpec((1, d), lambda i: (0, 0)),
                  _mod_spec(d, n_lat // ROW_TILE)],
        out_specs=pl.BlockSpec((ROW_TILE, d), lambda i: (i, 0)),
        out_shape=jax.ShapeDtypeStruct((n_rows, d), BF16),
        compiler_params=_params("arbitrary"),
        name="prenorm",
    )(x, g.reshape(1, d), mod)


def _gmm_kernel(tg_ref, nv_ref, lhs_ref, *rest, n_w, swiglu):
    w_refs, o_ref, wb_ref = rest[:n_w], rest[n_w], rest[n_w + 1]
    m = pl.program_id(1)
    first = jnp.logical_or(m == 0, tg_ref[m] != tg_ref[jnp.maximum(m - 1, 0)])

    @pl.when(first)
    def _():
        for i in range(n_w):
            wb_ref[i] = w_refs[i][0].astype(BF16)

    @pl.when(m < nv_ref[0])
    def _():
        x = lhs_ref[...]
        if swiglu:
            gate = jnp.dot(x, wb_ref[0], preferred_element_type=F32)
            up = jnp.dot(x, wb_ref[1], preferred_element_type=F32)
            out = _silu(gate) * up
        else:
            out = jnp.dot(x, wb_ref[0], preferred_element_type=F32)
        o_ref[...] = out.astype(o_ref.dtype)

    @pl.when(m >= nv_ref[0])
    def _():
        o_ref[...] = jnp.zeros(o_ref.shape, o_ref.dtype)


def _gmm(lhs, weights, tile_group, n_valid, *, tm, tn, out_dtype, swiglu=False, n_rows=None, name="gmm"):
    n_rows = lhs.shape[0] if n_rows is None else n_rows
    _, k, nw = weights[0].shape
    n_w = len(weights)
    lhs_cols = lhs.shape[1]
    grid_spec = pltpu.PrefetchScalarGridSpec(
        num_scalar_prefetch=2,
        grid=(nw // tn, n_rows // tm),
        in_specs=[pl.BlockSpec((tm, lhs_cols), lambda n, m, tg, nv: (m, 0))]
        + [pl.BlockSpec((1, k, tn), lambda n, m, tg, nv: (tg[m], 0, n)) for _ in range(n_w)],
        out_specs=pl.BlockSpec((tm, tn), lambda n, m, tg, nv: (m, n)),
        scratch_shapes=[pltpu.VMEM((n_w, k, tn), BF16)],
    )
    return pl.pallas_call(
        functools.partial(_gmm_kernel, n_w=n_w, swiglu=swiglu),
        grid_spec=grid_spec,
        out_shape=jax.ShapeDtypeStruct((n_rows, nw), out_dtype),
        compiler_params=_params("arbitrary", "arbitrary"),
        name=name,
    )(tile_group, n_valid, lhs, *weights)


def _dense_matmul(lhs, w, *, tn, out_dtype, swiglu=False, w2=None, n_rows=None, name="matmul"):
    n_rows = lhs.shape[0] if n_rows is None else n_rows
    tm = _largest_tile(n_rows, (1024, 768, 512, 256))
    n_tiles = n_rows // tm
    weights = [w[None]] if w2 is None else [w[None], w2[None]]
    return _gmm(lhs, weights, jnp.zeros((n_tiles,), jnp.int32), jnp.full((1,), n_tiles, jnp.int32),
                tm=tm, tn=tn, out_dtype=out_dtype, swiglu=swiglu, n_rows=n_rows, name=name)


def _rope_tables(n_lat, n_ctx, head_dim, reps):
    n_rows_grid = n_lat // GRID_W
    rows = jnp.repeat(jnp.arange(n_rows_grid), GRID_W).astype(F32)
    cols = jnp.tile(jnp.arange(GRID_W), n_rows_grid).astype(F32)
    axis_dim = head_dim // 2
    freqs = ROPE_THETA ** (-jnp.arange(0, axis_dim, 2, dtype=F32) / axis_dim)
    ra, ca = rows[:, None] * freqs, cols[:, None] * freqs
    rc, rs, cc, cs = jnp.cos(ra), jnp.sin(ra), jnp.cos(ca), jnp.sin(ca)
    z = jnp.zeros_like(rs)
    cos = jnp.concatenate([rc, rc, cc, cc], axis=-1)
    sin_up = jnp.concatenate([-rs, z, -cs, z], axis=-1)
    sin_dn = jnp.concatenate([z, rs, z, cs], axis=-1)
    pad = lambda t, v: jnp.concatenate(
        [jnp.tile(t, (1, reps)), jnp.full((n_ctx, LANES), v, F32)], axis=0)
    return pad(cos, 1.0), pad(sin_up, 0.0), pad(sin_dn, 0.0)


def _rope(x, cos, sin_up, sin_dn, quarter):
    return (x * cos + pltpu.roll(x, LANES - quarter, 1) * sin_up
            + pltpu.roll(x, quarter, 1) * sin_dn)


def _head_rmsnorm(x, g):
    return x * lax.rsqrt(jnp.mean(x * x, axis=-1, keepdims=True) + NORM_EPS) * g


def _qkpost_kernel(p_ref, gq_ref, gk_ref, ca_ref, ua_ref, da_ref, cb_ref, ub_ref, db_ref,
                   q_ref, k_ref, v_ref, u_ref):
    def col(c):
        return p_ref[:, c * LANES:(c + 1) * LANES]

    def put(ref, c, x):
        ref[:, c * LANES:(c + 1) * LANES] = x.astype(BF16)

    rope_a = lambda x: _rope(x, ca_ref[...], ua_ref[...], da_ref[...], HEAD_DIM_A // 4)
    rope_b = lambda x: _rope(x, cb_ref[...], ub_ref[...], db_ref[...], HEAD_DIM_B // 4)
    scale_a, scale_b = LOG2_E * HEAD_DIM_A ** -0.5, LOG2_E * HEAD_DIM_B ** -0.5
    c0 = 0
    for h in range(N_HEADS_A):
        put(q_ref, h, rope_a(_head_rmsnorm(col(c0 + h), gq_ref[...])) * scale_a)
    c0 += N_HEADS_A
    for h in range(N_HEADS_B):
        put(q_ref, N_HEADS_A + h, rope_b(col(c0 + h)) * scale_b)
    c0 += N_HEADS_B
    for g in range(N_FOURIER_GROUPS):
        put(u_ref, g, col(c0 + g))
    c0 += N_FOURIER_GROUPS
    for h in range(N_KV_HEADS_A):
        put(k_ref, h, rope_a(_head_rmsnorm(col(c0 + h), gk_ref[...])))
    c0 += N_KV_HEADS_A
    for h in range(N_KV_HEADS_A):
        put(v_ref, h, col(c0 + h))
    c0 += N_KV_HEADS_A
    for h in range(N_HEADS_B):
        put(k_ref, N_KV_HEADS_A + h, rope_b(col(c0 + h)))
    c0 += N_HEADS_B
    for h in range(N_HEADS_B):
        put(v_ref, N_KV_HEADS_A + h, col(c0 + h))


Q_COLS = QA_COLS + QB_COLS
K_COLS = KA_COLS + KB_COLS
V_COLS = VA_COLS + VB_COLS


def _qkpost(p, g_q, g_k, rope_a, rope_b):
    t = p.shape[0]
    row = lambda w: pl.BlockSpec((ROW_TILE, w), lambda i: (i, 0))
    vec = pl.BlockSpec((1, LANES), lambda i: (0, 0))
    widths = (Q_COLS, K_COLS, V_COLS, C_COLS)
    return pl.pallas_call(
        _qkpost_kernel,
        grid=(t // ROW_TILE,),
        in_specs=[row(IN_COLS), vec, vec] + [row(LANES)] * 6,
        out_specs=[row(w) for w in widths],
        out_shape=[jax.ShapeDtypeStruct((t, w), BF16) for w in widths],
        compiler_params=_params("arbitrary"),
        name="qk_post",
    )(p, g_q.reshape(1, LANES), g_k.reshape(1, LANES), *rope_a, *rope_b)


ATTN_TQ = 256
ATTN_TK = 512


REDUCE_ROWS = 32


def _stream_reduce(op, x):
    rows = x.shape[0]
    step = REDUCE_ROWS if rows % REDUCE_ROWS == 0 else rows
    acc = x[:step]
    for r in range(step, rows, step):
        acc = op(acc, x[r:r + step])
    red = jnp.max if op is jnp.maximum else jnp.sum
    return red(acc, axis=0, keepdims=True)


ATTN_TILES = 4


class _Tile:
    def __init__(self, refs, kv):
        self.qt, self.m, self.l, self.acc, s_a, c_a, s_b, c_b = refs
        self.stage = ((s_a, c_a), (s_b, c_b))
        self.kv = slice(kv * LANES, (kv + 1) * LANES)


def _attn_scratch():
    stat = pltpu.VMEM((1, ATTN_TQ), F32)
    score = pltpu.VMEM((ATTN_TK, ATTN_TQ), F32)
    per_tile = [pltpu.VMEM((LANES, ATTN_TQ), BF16), stat, stat, pltpu.VMEM((LANES, ATTN_TQ), F32),
                score, stat, score, stat]
    return per_tile * ATTN_TILES


def _make_tiles(scratch, kv_of_tile):
    n = len(scratch) // ATTN_TILES
    return [_Tile(scratch[n * i:n * (i + 1)], kv) for i, kv in enumerate(kv_of_tile)]


def _flash_t(tiles, k_ref, vtl_ref, vtc_ref, *, n_lat, n_ctx, is_lat):
    n_chunks = n_lat // ATTN_TK
    for t in tiles:
        t.m[...] = jnp.full(t.m.shape, NEG_BIG, F32)
        t.l[...] = jnp.zeros(t.l.shape, F32)
        t.acc[...] = jnp.zeros(t.acc.shape, F32)

    def scores(t, k, stage, rows):
        s = jnp.dot(k, t.qt[...], preferred_element_type=F32)
        s_ref, c_ref = t.stage[stage]
        s_ref[0:rows, :] = s
        c_ref[...] = _stream_reduce(jnp.maximum, s)

    def scores_lat(j, stage):
        off = pl.multiple_of(j * ATTN_TK, ATTN_TK)
        for t in tiles:
            scores(t, k_ref[pl.ds(off, ATTN_TK), t.kv], stage, ATTN_TK)

    def scores_ctx():
        for t in tiles:
            scores(t, k_ref[n_lat:n_lat + n_ctx, t.kv], 0, n_ctx)

    def consume(t, vt, stage, rows):
        s_ref, c_ref = t.stage[stage]
        m_old = t.m[...]
        m_new = jnp.maximum(m_old, c_ref[...])
        alpha = jnp.exp2(m_old - m_new)
        p = jnp.exp2(s_ref[0:rows, :] - m_new)
        t.l[...] = alpha * t.l[...] + _stream_reduce(jnp.add, p)
        t.acc[...] = alpha * t.acc[...] + jnp.dot(vt, p.astype(BF16), preferred_element_type=F32)
        t.m[...] = m_new

    def consume_lat(j, stage):
        for t in tiles:
            consume(t, vtl_ref[j, t.kv, :], stage, ATTN_TK)

    @pl.when(is_lat)
    def _():
        scores_lat(0, 0)

        def body(i, carry):
            j = 2 * i
            scores_lat(j + 1, 1)
            consume_lat(j, 0)
            scores_lat(j + 2, 0)
            consume_lat(j + 1, 1)
            return carry
        lax.fori_loop(0, n_chunks // 2 - 1, body, 0)
        scores_lat(n_chunks - 1, 1)
        consume_lat(n_chunks - 2, 0)
        scores_ctx()
        consume_lat(n_chunks - 1, 1)

    @pl.when(jnp.logical_not(is_lat))
    def _():
        scores_ctx()

    for t in tiles:
        consume(t, vtc_ref[t.kv, :], 0, n_ctx)


def _kv_specs(t, n_chunks, n_ctx, width, first_blk):
    return [pl.BlockSpec((t, width), lambda g, i: (0, first_blk + g)),
            pl.BlockSpec((n_chunks, width, ATTN_TK), lambda g, i: (0, first_blk + g, 0)),
            pl.BlockSpec((width, n_ctx), lambda g, i: (first_blk + g, 0))]


def _gqa_kernel(q_ref, k_ref, vtl_ref, vtc_ref, o_ref, *scratch, n_lat, n_ctx):
    tiles = _make_tiles(scratch, [0] * ATTN_TILES)
    for h, t in enumerate(tiles):
        t.qt[...] = q_ref[:, h * LANES:(h + 1) * LANES].astype(F32).T.astype(BF16)
    is_lat = pl.program_id(1) < n_lat // ATTN_TQ
    _flash_t(tiles, k_ref, vtl_ref, vtc_ref, n_lat=n_lat, n_ctx=n_ctx, is_lat=is_lat)
    for h, t in enumerate(tiles):
        o_ref[:, h * LANES:(h + 1) * LANES] = (t.acc[...] / t.l[...]).T.astype(o_ref.dtype)


def _gqa_attention(q, k, vt_lat, vt_ctx, n_lat, n_ctx, n_q_rows):
    group = N_HEADS_A // N_KV_HEADS_A
    assert group == ATTN_TILES
    return pl.pallas_call(
        functools.partial(_gqa_kernel, n_lat=n_lat, n_ctx=n_ctx),
        grid=(N_KV_HEADS_A, n_q_rows // ATTN_TQ),
        in_specs=[pl.BlockSpec((ATTN_TQ, group * LANES), lambda g, i: (i, g))]
        + _kv_specs(n_lat + n_ctx, n_lat // ATTN_TK, n_ctx, LANES, 0),
        out_specs=pl.BlockSpec((ATTN_TQ, group * LANES), lambda g, i: (i, g)),
        out_shape=jax.ShapeDtypeStruct((n_q_rows, QA_COLS), BF16),
        scratch_shapes=_attn_scratch(),
        compiler_params=_params("arbitrary", "arbitrary"),
        name="gqa_attention",
    )(q, k, vt_lat, vt_ctx)


DIFF_HEADS_PER_STEP = 2


def _diff_kernel(q_ref, k_ref, vtl_ref, vtc_ref, lam_ref, gs_ref, o_ref, *scratch,
                 n_lat, n_ctx, lambda_init):
    tiles = _make_tiles(scratch, [0, 0, 1, 1])
    q = q_ref[...].astype(F32)
    lane = lax.broadcasted_iota(jnp.int32, (ATTN_TQ, LANES), 1)
    for h in range(DIFF_HEADS_PER_STEP):
        qh = q[:, h * LANES:(h + 1) * LANES]
        zero = jnp.zeros_like(qh)
        tiles[2 * h].qt[...] = jnp.where(lane < HEAD_DIM_B, qh, zero).T.astype(BF16)
        tiles[2 * h + 1].qt[...] = jnp.where(lane >= HEAD_DIM_B, qh, zero).T.astype(BF16)
    is_lat = pl.program_id(1) < n_lat // ATTN_TQ
    _flash_t(tiles, k_ref, vtl_ref, vtc_ref, n_lat=n_lat, n_ctx=n_ctx, is_lat=is_lat)
    lv = lam_ref[...]
    lam = (jnp.exp(jnp.sum(lv[0:1] * lv[1:2], axis=-1, keepdims=True))
           - jnp.exp(jnp.sum(lv[2:3] * lv[3:4], axis=-1, keepdims=True)) + lambda_init)
    for h in range(DIFF_HEADS_PER_STEP):
        out1, out2 = (t.acc[...] / t.l[...] for t in tiles[2 * h:2 * h + 2])
        diff = (out1 - lam * out2).T
        o_ref[:, h * LANES:(h + 1) * LANES] = (
            _head_rmsnorm(diff, gs_ref[...]) * (1.0 - lambda_init)).astype(o_ref.dtype)


def _diff_attention(q, k, vt_lat, vt_ctx, lam_vecs, g_subln, lambda_init, n_lat, n_ctx, n_q_rows):
    width = DIFF_HEADS_PER_STEP * LANES
    assert 2 * DIFF_HEADS_PER_STEP == ATTN_TILES and QA_COLS % width == 0 and KA_COLS % width == 0
    return pl.pallas_call(
        functools.partial(_diff_kernel, n_lat=n_lat, n_ctx=n_ctx, lambda_init=lambda_init),
        grid=(N_HEADS_B // DIFF_HEADS_PER_STEP, n_q_rows // ATTN_TQ),
        in_specs=[pl.BlockSpec((ATTN_TQ, width), lambda h, i: (i, QA_COLS // width + h))]
        + _kv_specs(n_lat + n_ctx, n_lat // ATTN_TK, n_ctx, width, KA_COLS // width)
        + [pl.BlockSpec((4, HEAD_DIM_B), lambda h, i: (0, 0)),
           pl.BlockSpec((1, LANES), lambda h, i: (0, 0))],
        out_specs=pl.BlockSpec((ATTN_TQ, width), lambda h, i: (i, h)),
        out_shape=jax.ShapeDtypeStruct((n_q_rows, VB_COLS), BF16),
        scratch_shapes=_attn_scratch(),
        compiler_params=_params("arbitrary", "arbitrary"),
        name="diff_attention",
    )(q, k, vt_lat, vt_ctx, lam_vecs, g_subln.reshape(1, LANES))


def _dft_cos_sin(n):
    j = np.arange(n, dtype=np.float64)
    ang = 2.0 * np.pi * np.outer(j, j) / n
    return np.cos(ang), np.sin(ang)


def _channel_mix_weights(cs_ref, wf_ref, ab_ref):
    cs = cs_ref[...].astype(BF16)
    for g in range(N_FOURIER_GROUPS):
        ab_ref[g] = jnp.dot(cs, wf_ref[g].astype(BF16), preferred_element_type=F32).astype(BF16)


def _channel_mix(xr, xi, ab_ref, norm):
    outs = []
    for g in range(N_FOURIER_GROUPS):
        sl = slice(g * LANES, (g + 1) * LANES)
        lhs = jnp.concatenate([xr[:, sl], xi[:, sl]], axis=-1).astype(BF16)
        outs.append(jnp.dot(lhs, ab_ref[g], preferred_element_type=F32))
    return jnp.concatenate(outs, axis=-1) * norm


def _fourier_stage1_kernel(w_ref, u_ref, o_ref):
    o_ref[...] = jnp.dot(w_ref[...].astype(BF16), u_ref[...],
                         preferred_element_type=F32).astype(o_ref.dtype)


def _fourier_stage2_kernel(a_ref, tc_ref, ts_ref, w3_ref, cs_ref, wf_ref, o_ref, ab_ref, *, norm):
    @pl.when(pl.program_id(0) == 0)
    def _():
        _channel_mix_weights(cs_ref, wf_ref, ab_ref)

    ar = a_ref[0, 0].astype(F32)
    ai = a_ref[1, 0].astype(F32)
    tc, ts = tc_ref[0], ts_ref[0]
    br = ar * tc + ai * ts
    bi = ai * tc - ar * ts
    stack = jnp.concatenate([br, bi], axis=0).astype(BF16)
    x = jnp.dot(w3_ref[...].astype(BF16), stack, preferred_element_type=F32)
    half = x.shape[0] // 2
    o_ref[...] = _channel_mix(x[:half], x[half:], ab_ref, norm).astype(o_ref.dtype)


def _fourier_ctx_kernel(u_ref, w_ref, cs_ref, wf_ref, o_ref, ab_ref, *, norm):
    _channel_mix_weights(cs_ref, wf_ref, ab_ref)
    x = jnp.dot(w_ref[...].astype(BF16), u_ref[...], preferred_element_type=F32)
    half = x.shape[0] // 2
    o_ref[...] = _channel_mix(x[:half], x[half:], ab_ref, norm).astype(o_ref.dtype)


def _fourier_latent(u_lat, w_fourier):
    n = u_lat.shape[0]
    n2 = LANES
    n1 = n // n2
    c1, s1 = _dft_cos_sin(n1)
    w1 = jnp.asarray(np.concatenate([c1, -s1], axis=0), F32)
    k1n2 = 2.0 * np.pi * np.outer(np.arange(n1), np.arange(n2)) / n
    tw_c = jnp.asarray(np.cos(k1n2)[:, :, None], F32)
    tw_s = jnp.asarray(np.sin(k1n2)[:, :, None], F32)
    c2, s2 = _dft_cos_sin(n2)
    w3 = jnp.asarray(np.block([[c2, s2], [-s2, c2]]), F32)
    cc, sc = _dft_cos_sin(FOURIER_GROUP_DIM)
    cs = jnp.asarray(np.concatenate([cc, sc], axis=0), F32)
    width = n2 * C_COLS
    cb = _largest_tile(width, (4096, 2048, 1024, 512))
    stage1 = pl.pallas_call(
        _fourier_stage1_kernel,
        grid=(width // cb,),
        in_specs=[pl.BlockSpec((2 * n1, n1), lambda j: (0, 0)),
                  pl.BlockSpec((n1, cb), lambda j: (0, j))],
        out_specs=pl.BlockSpec((2 * n1, cb), lambda j: (0, j)),
        out_shape=jax.ShapeDtypeStruct((2 * n1, width), BF16),
        compiler_params=_params("arbitrary"),
        name="fourier_stage1",
    )(w1, u_lat.reshape(n1, width))
    a = stage1.reshape(2, n1, n2, C_COLS)
    norm = 1.0 / math.sqrt(n * FOURIER_GROUP_DIM)
    out = pl.pallas_call(
        functools.partial(_fourier_stage2_kernel, norm=norm),
        grid=(n1,),
        in_specs=[pl.BlockSpec((2, 1, n2, C_COLS), lambda k: (0, k, 0, 0)),
                  pl.BlockSpec((1, n2, 1), lambda k: (k, 0, 0)),
                  pl.BlockSpec((1, n2, 1), lambda k: (k, 0, 0)),
                  pl.BlockSpec((2 * n2, 2 * n2), lambda k: (0, 0)),
                  pl.BlockSpec((2 * FOURIER_GROUP_DIM, FOURIER_GROUP_DIM), lambda k: (0, 0)),
                  pl.BlockSpec((N_FOURIER_GROUPS, FOURIER_GROUP_DIM, FOURIER_GROUP_DIM),
                               lambda k: (0, 0, 0))],
        out_specs=pl.BlockSpec((n2, C_COLS), lambda k: (0, k)),
        out_shape=jax.ShapeDtypeStruct((n2, n1 * C_COLS), BF16),
        scratch_shapes=[pltpu.VMEM((N_FOURIER_GROUPS, 2 * FOURIER_GROUP_DIM, FOURIER_GROUP_DIM), BF16)],
        compiler_params=_params("arbitrary"),
        name="fourier_stage2",
    )(a, tw_c, tw_s, w3, cs, w_fourier)
    return out.reshape(n, C_COLS)


def _fourier_ctx(u_ctx, w_fourier):
    m = u_ctx.shape[0]
    c, s = _dft_cos_sin(m)
    w = jnp.asarray(np.concatenate([c, -s], axis=0), F32)
    cc, sc = _dft_cos_sin(FOURIER_GROUP_DIM)
    cs = jnp.asarray(np.concatenate([cc, sc], axis=0), F32)
    norm = 1.0 / math.sqrt(m * FOURIER_GROUP_DIM)
    return pl.pallas_call(
        functools.partial(_fourier_ctx_kernel, norm=norm),
        out_shape=jax.ShapeDtypeStruct((m, C_COLS), BF16),
        scratch_shapes=[pltpu.VMEM((N_FOURIER_GROUPS, 2 * FOURIER_GROUP_DIM, FOURIER_GROUP_DIM), BF16)],
        compiler_params=pltpu.CompilerParams(vmem_limit_bytes=VMEM_LIMIT),
        name="fourier_ctx",
    )(u_ctx, w, cs, w_fourier)


def _gated_residual(x, f, g, mod, gate_row):
    ms = jnp.mean(f * f, axis=-1, keepdims=True)
    return x + mod[gate_row:gate_row + 1] * (f * lax.rsqrt(ms + NORM_EPS) * g)


def _residual_kernel(x_ref, f_ref, g_ref, mod_ref, o_ref, *, gate_row):
    o_ref[...] = _gated_residual(x_ref[...], f_ref[...], g_ref[...], mod_ref[0], gate_row)


def _post_residual(x, f, g, mod, gate_row, n_rows, n_lat):
    d = x.shape[1]
    row = pl.BlockSpec((ROW_TILE, d), lambda i: (i, 0))
    return pl.pallas_call(
        functools.partial(_residual_kernel, gate_row=gate_row),
        grid=(n_rows // ROW_TILE,),
        in_specs=[row, row, pl.BlockSpec((1, d), lambda i: (0, 0)), _mod_spec(d, n_lat // ROW_TILE)],
        out_specs=row,
        out_shape=jax.ShapeDtypeStruct((n_rows, d), F32),
        compiler_params=_params("arbitrary"),
        name="post_residual",
    )(x, f, g.reshape(1, d), mod)


MOE_TM = 512
R_IDX, R_W, R_RANK = 0, 2, 4


def _router_kernel(x_ref, g_ref, mod_ref, wr_ref, h_ref, route_ref, cnt_ref, carry_ref, *, shift_row):
    @pl.when(pl.program_id(0) == 0)
    def _():
        carry_ref[...] = jnp.zeros(carry_ref.shape, F32)

    h = _norm_modulate(x_ref[...], g_ref[...], mod_ref[0], shift_row)
    h_ref[...] = h
    logits = jnp.dot(h, wr_ref[...], preferred_element_type=F32, precision=lax.Precision.HIGHEST)
    rows = logits.shape[0]
    lane = lax.broadcasted_iota(jnp.int32, logits.shape, 1).astype(F32)
    neg = jnp.full(logits.shape, -jnp.inf, F32)
    l1 = jnp.where(lane < N_EXPERTS, logits, neg)
    m1 = jnp.max(l1, axis=-1, keepdims=True)
    i1 = jnp.min(jnp.where(l1 == m1, lane, float(LANES)), axis=-1, keepdims=True)
    l2 = jnp.where(lane == i1, neg, l1)
    m2 = jnp.max(l2, axis=-1, keepdims=True)
    i2 = jnp.min(jnp.where(l2 == m2, lane, float(LANES)), axis=-1, keepdims=True)
    e2 = jnp.exp(m2 - m1)
    w1 = 1.0 / (1.0 + e2)
    w2 = e2 / (1.0 + e2)
    hit1 = lane == i1
    hit2 = lane == i2
    onehot = jnp.logical_or(hit1, hit2).astype(BF16)
    r_i = lax.broadcasted_iota(jnp.int32, (rows, rows), 0)
    c_i = lax.broadcasted_iota(jnp.int32, (rows, rows), 1)
    before = (c_i < r_i).astype(BF16)
    base = carry_ref[...] + jnp.dot(before, onehot, preferred_element_type=F32)
    rank1 = jnp.sum(jnp.where(hit1, base, 0.0), axis=-1, keepdims=True)
    rank2 = jnp.sum(jnp.where(hit2, base, 0.0), axis=-1, keepdims=True)
    carry_ref[...] = carry_ref[...] + jnp.sum(onehot.astype(F32), axis=0, keepdims=True)
    rec = jnp.zeros(logits.shape, F32)
    for off, val in ((R_IDX, i1), (R_IDX + 1, i2), (R_W, w1), (R_W + 1, w2),
                     (R_RANK, rank1), (R_RANK + 1, rank2)):
        rec = jnp.where(lane == off, val, rec)
    route_ref[...] = rec
    cnt_ref[...] = jnp.broadcast_to(carry_ref[...], cnt_ref.shape)


def _router(x, g, mod, w_router, n_lat):
    d = x.shape[1]
    wr = jnp.zeros((d, LANES), F32).at[:, :N_EXPERTS].set(w_router)
    return pl.pallas_call(
        functools.partial(_router_kernel, shift_row=3),
        grid=(n_lat // ROW_TILE,),
        in_specs=[pl.BlockSpec((ROW_TILE, d), lambda i: (i, 0)),
                  pl.BlockSpec((1, d), lambda i: (0, 0)),
                  pl.BlockSpec((1, N_MOD, d), lambda i: (0, 0, 0)),
                  pl.BlockSpec((d, LANES), lambda i: (0, 0))],
        out_specs=[pl.BlockSpec((ROW_TILE, d), lambda i: (i, 0)),
                   pl.BlockSpec((ROW_TILE, LANES), lambda i: (i, 0)),
                   pl.BlockSpec((8, LANES), lambda i: (0, 0))],
        out_shape=[jax.ShapeDtypeStruct((n_lat, d), F32),
                   jax.ShapeDtypeStruct((n_lat, LANES), F32),
                   jax.ShapeDtypeStruct((8, LANES), F32)],
        scratch_shapes=[pltpu.VMEM((1, LANES), F32)],
        compiler_params=_params("arbitrary"),
        name="router",
    )(x, g.reshape(1, d), mod, wr)


def _row_copy(src_hbm, src_row, dst_ref, dst_row, sem):
    return pltpu.make_async_copy(src_hbm.at[pl.ds(src_row, 1)], dst_ref.at[pl.ds(dst_row, 1)], sem)


def _dispatch_kernel(src_ref, nv_ref, h_hbm, o_ref, buf_ref, sem):
    i = pl.program_id(0)
    base = i * MOE_TM

    @pl.when(i < nv_ref[0])
    def _():
        def issue(r, carry):
            _row_copy(h_hbm, src_ref[base + r], buf_ref, r, sem).start()
            return carry
        lax.fori_loop(0, MOE_TM, issue, 0)

        def drain(r, carry):
            _row_copy(h_hbm, 0, buf_ref, 0, sem).wait()
            return carry
        lax.fori_loop(0, MOE_TM, drain, 0)
        o_ref[...] = buf_ref[...].astype(o_ref.dtype)

    @pl.when(i >= nv_ref[0])
    def _():
        o_ref[...] = jnp.zeros(o_ref.shape, o_ref.dtype)


def _dispatch(src_token, n_valid, h):
    d = h.shape[1]
    n_slots = src_token.shape[0]
    grid_spec = pltpu.PrefetchScalarGridSpec(
        num_scalar_prefetch=2,
        grid=(n_slots // MOE_TM,),
        in_specs=[pl.BlockSpec(memory_space=pl.ANY)],
        out_specs=pl.BlockSpec((MOE_TM, d), lambda i, src, nv: (i, 0)),
        scratch_shapes=[pltpu.VMEM((MOE_TM, d), F32), pltpu.SemaphoreType.DMA(())],
    )
    return pl.pallas_call(
        _dispatch_kernel,
        grid_spec=grid_spec,
        out_shape=jax.ShapeDtypeStruct((n_slots, d), BF16),
        compiler_params=_params("arbitrary"),
        name="moe_dispatch",
    )(src_token, n_valid, h)


def _combine_kernel(dest_ref, y_hbm, route_ref, x_ref, g_ref, mod_ref, o_ref, buf_ref, sem, *, gate_row):
    base = pl.program_id(0) * ROW_TILE

    def issue(r, carry):
        for k in range(TOP_K):
            _row_copy(y_hbm, dest_ref[TOP_K * (base + r) + k], buf_ref.at[k], r, sem).start()
        return carry
    lax.fori_loop(0, ROW_TILE, issue, 0)

    def drain(r, carry):
        for k in range(TOP_K):
            _row_copy(y_hbm, 0, buf_ref.at[k], 0, sem).wait()
        return carry
    lax.fori_loop(0, ROW_TILE, drain, 0)

    route = route_ref[...]
    f = route[:, R_W:R_W + 1] * buf_ref[0] + route[:, R_W + 1:R_W + 2] * buf_ref[1]
    o_ref[...] = _gated_residual(x_ref[...], f, g_ref[...], mod_ref[0], gate_row)


def _combine_residual(dest_flat, y_sorted, route, x, g, mod, gate_row, n_lat):
    d = x.shape[1]
    row = lambda w: pl.BlockSpec((ROW_TILE, w), lambda i, dest: (i, 0))
    grid_spec = pltpu.PrefetchScalarGridSpec(
        num_scalar_prefetch=1,
        grid=(n_lat // ROW_TILE,),
        in_specs=[pl.BlockSpec(memory_space=pl.ANY), row(LANES), row(d),
                  pl.BlockSpec((1, d), lambda i, dest: (0, 0)),
                  pl.BlockSpec((1, N_MOD, d), lambda i, dest: (0, 0, 0))],
        out_specs=row(d),
        scratch_shapes=[pltpu.VMEM((TOP_K, ROW_TILE, d), F32), pltpu.SemaphoreType.DMA(())],
    )
    return pl.pallas_call(
        functools.partial(_combine_kernel, gate_row=gate_row),
        grid_spec=grid_spec,
        out_shape=jax.ShapeDtypeStruct((n_lat, d), F32),
        compiler_params=_params("arbitrary"),
        name="moe_combine",
    )(dest_flat, y_sorted, route, x, g.reshape(1, d), mod)


def _moe_layer(x, g_pre, g_post, mod, w_router, w_gate, w_up, w_down, n_lat):
    d_ff = w_gate.shape[-1]
    h, route, cnt = _router(x, g_pre, mod, w_router, n_lat)
    experts = route[:, R_IDX:R_IDX + TOP_K].astype(jnp.int32)
    ranks = route[:, R_RANK:R_RANK + TOP_K].astype(jnp.int32)
    counts = cnt[0, :N_EXPERTS].astype(jnp.int32)
    padded = ((counts + MOE_TM - 1) // MOE_TM) * MOE_TM
    ends = jnp.cumsum(padded)
    starts = ends - padded
    dest = (starts[experts] + ranks).reshape(-1)
    n_tiles = TOP_K * n_lat // MOE_TM + N_EXPERTS
    n_slots = n_tiles * MOE_TM
    tile_start = jnp.arange(n_tiles, dtype=jnp.int32) * MOE_TM
    tile_group = jnp.minimum(jnp.sum((tile_start[:, None] >= ends[None, :]).astype(jnp.int32), axis=1),
                             N_EXPERTS - 1)
    n_valid = (ends[-1:] // MOE_TM).astype(jnp.int32)

    token = jnp.repeat(jnp.arange(n_lat, dtype=jnp.int32), TOP_K)
    src_token = jnp.zeros((n_slots,), jnp.int32).at[dest].set(token)
    hs = _dispatch(src_token, n_valid, h)
    act = _gmm(hs, [w_gate, w_up], tile_group, n_valid, tm=MOE_TM,
               tn=_largest_tile(d_ff, (512, 256, 128)), out_dtype=BF16, swiglu=True, name="moe_gate_up")
    y = _gmm(act, [w_down], tile_group, n_valid, tm=MOE_TM,
             tn=_largest_tile(x.shape[1], (256, 128)), out_dtype=F32, name="moe_down")
    return _combine_residual(dest, y, route, x, g_post, mod, 5, n_lat)


def kernel(x, c, ctx, c_ctx, w_mod, b_mod, g_pre_mix, g_post_mix, g_pre_ffn, g_post_ffn, w_in, g_q_a, g_k_a, lambda_q1, lambda_k1, lambda_q2, lambda_k2, g_subln_b, w_fourier, w_out, w_gate_dense, w_up_dense, w_down_dense, w_router, w_gate_moe, w_up_moe, w_down_moe):
    batch, n_lat, d = x.shape
    n_ctx = ctx.shape[1]
    depth = w_mod.shape[0]
    assert batch == 1 and n_ctx == ROW_TILE and n_lat % max(2 * ATTN_TK, MOE_TM, GRID_W) == 0
    assert depth == 2, "layer 0 dense, layer 1 (last) routed experts"
    t = n_lat + n_ctx

    cond_rows = jnp.concatenate([c, c_ctx[None, :], jnp.zeros((6, d), F32)], axis=0)
    mod_all = _modulation(cond_rows, w_mod, b_mod)
    rope_a = _rope_tables(n_lat, n_ctx, HEAD_DIM_A, 1)
    rope_b = _rope_tables(n_lat, n_ctx, HEAD_DIM_B, 2)

    xs = jnp.concatenate([x[0], ctx[0]], axis=0)
    for layer in range(depth):
        last = layer == depth - 1
        n_out = n_lat if last else t
        mod = mod_all[layer, :2].reshape(2, N_MOD, d)
        lambda_init = 0.8 - 0.6 * math.exp(-0.3 * layer)

        h = _prenorm(xs, g_pre_mix[layer], mod, 0, t, n_lat)
        p = _dense_matmul(h, w_in[layer], tn=512, out_dtype=F32, name="in_proj")
        q, k, v, u = _qkpost(p, g_q_a[layer], g_k_a[layer], rope_a, rope_b)
        vt_lat = v[:n_lat].reshape(n_lat // ATTN_TK, ATTN_TK, V_COLS).transpose(0, 2, 1)
        vt_ctx = v[n_lat:].T
        y_a = _gqa_attention(q, k, vt_lat, vt_ctx, n_lat, n_ctx, n_out)
        lam_vecs = jnp.stack([lambda_q1[layer], lambda_k1[layer], lambda_q2[layer], lambda_k2[layer]])
        y_b = _diff_attention(q, k, vt_lat, vt_ctx, lam_vecs, g_subln_b[layer], lambda_init,
                              n_lat, n_ctx, n_out)
        y_c = _fourier_latent(u[:n_lat], w_fourier[layer])
        if not last:
            y_c = jnp.concatenate([y_c, _fourier_ctx(u[n_lat:], w_fourier[layer])], axis=0)
        y = jnp.concatenate([y_a, y_b, y_c], axis=-1)
        f = _dense_matmul(y, w_out[layer], tn=512, out_dtype=F32, name="out_proj")
        xs = _post_residual(xs, f, g_post_mix[layer], mod, 2, n_out, n_lat)

        j = layer // 2
        if layer % 2 == 0:
            h = _prenorm(xs, g_pre_ffn[layer], mod, 3, n_out, n_lat)
            act = _dense_matmul(h, w_gate_dense[j], w2=w_up_dense[j], swiglu=True, tn=512,
                                out_dtype=BF16, name="ffn_gate_up")
            f = _dense_matmul(act, w_down_dense[j], tn=256, out_dtype=F32, name="ffn_down")
            xs = _post_residual(xs, f, g_post_ffn[layer], mod, 5, n_out, n_lat)
        else:
            xs = _moe_layer(xs, g_pre_ffn[layer], g_post_ffn[layer], mod, w_router[j],
                            w_gate_moe[j], w_up_moe[j], w_down_moe[j], n_lat)
    return xs[:n_lat][None]
```

```python
import functools
import math

import numpy as np
import jax
import jax.numpy as jnp
from jax import lax
from jax.experimental import pallas as pl
from jax.experimental.pallas import tpu as pltpu

F32 = jnp.float32
BF16 = jnp.bfloat16

GRID_W = 64
ROPE_THETA = 10000.0
NORM_EPS = 1e-6
N_HEADS_A, N_KV_HEADS_A, HEAD_DIM_A = 8, 2, 128
N_HEADS_B, HEAD_DIM_B = 4, 64
VALUE_DIM_B = 2 * HEAD_DIM_B
N_FOURIER_GROUPS, FOURIER_GROUP_DIM = 4, 128
QA_COLS = N_HEADS_A * HEAD_DIM_A
QB_COLS = N_HEADS_B * 2 * HEAD_DIM_B
C_COLS = N_FOURIER_GROUPS * FOURIER_GROUP_DIM
KA_COLS = N_KV_HEADS_A * HEAD_DIM_A
VA_COLS = KA_COLS
KB_COLS = QB_COLS
VB_COLS = N_HEADS_B * VALUE_DIM_B
KV_START = QA_COLS + QB_COLS + C_COLS
IN_COLS = KV_START + KA_COLS + VA_COLS + KB_COLS + VB_COLS
N_EXPERTS, TOP_K = 8, 2
N_MOD = 6

LANES = 128
V7X_VMEM_BYTES = 64 * 1024 * 1024
VMEM_LIMIT = 52 * 1024 * 1024

ROW_TILE = 256
NEG_BIG = -1e30
LOG2_E = math.log2(math.e)


def _params(*sem):
    return pltpu.CompilerParams(dimension_semantics=sem, vmem_limit_bytes=VMEM_LIMIT)


def _silu(x):
    return x / (1.0 + jnp.exp(-x))


def _largest_tile(n, candidates):
    for c in candidates:
        if n % c == 0:
            return c
    raise ValueError(f"no tile in {candidates} divides {n}")


def _mod_kernel(c_ref, w_ref, b_ref, o_ref):
    cond = _silu(c_ref[...])
    o_ref[0] = jnp.dot(cond.astype(BF16), w_ref[0].astype(BF16),
                       preferred_element_type=F32) + b_ref[0]


def _modulation(cond_rows, w_mod, b_mod):
    depth, d, n6 = w_mod.shape
    tn = _largest_tile(n6, (1024, 512, 256, 128))
    return pl.pallas_call(
        _mod_kernel,
        grid=(depth, n6 // tn),
        in_specs=[pl.BlockSpec((8, d), lambda l, j: (0, 0)),
                  pl.BlockSpec((1, d, tn), lambda l, j: (l, 0, j)),
                  pl.BlockSpec((1, 1, tn), lambda l, j: (l, 0, j))],
        out_specs=pl.BlockSpec((1, 8, tn), lambda l, j: (l, 0, j)),
        out_shape=jax.ShapeDtypeStruct((depth, 8, n6), F32),
        compiler_params=_params("arbitrary", "arbitrary"),
        name="modulation",
    )(cond_rows, w_mod, b_mod.reshape(depth, 1, n6))


def _norm_modulate(x, g, mod, shift_row):
    ms = jnp.mean(x * x, axis=-1, keepdims=True)
    y = x * lax.rsqrt(ms + NORM_EPS) * g
    return y * (1.0 + mod[shift_row + 1:shift_row + 2]) + mod[shift_row:shift_row + 1]


def _prenorm_kernel(x_ref, g_ref, mod_ref, o_ref, *, shift_row):
    h = _norm_modulate(x_ref[...], g_ref[...], mod_ref[0], shift_row)
    o_ref[...] = h.astype(o_ref.dtype)


def _mod_spec(d, n_lat_tiles):
    return pl.BlockSpec((1, N_MOD, d), lambda i: ((i >= n_lat_tiles).astype(jnp.int32), 0, 0))


def _prenorm(x, g, mod, shift_row, n_rows, n_lat):
    d = x.shape[1]
    return pl.pallas_call(
        functools.partial(_prenorm_kernel, shift_row=shift_row),
        grid=(n_rows // ROW_TILE,),
        in_specs=[pl.BlockSpec((ROW_TILE, d), lambda i: (i, 0)),
                  pl.BlockSpec((1, d), lambda i: (0, 0)),
                  _mod_spec(d, n_lat // ROW_TILE)],
        out_specs=pl.BlockSpec((ROW_TILE, d), lambda i: (i, 0)),
        out_shape=jax.ShapeDtypeStruct((n_rows, d), BF16),
        compiler_params=_params("arbitrary"),
        name="prenorm",
    )(x, g.reshape(1, d), mod)


def _gmm_kernel(tg_ref, rows_ref, *refs, n_lhs, n_w, swiglu, sub):
    lhs_refs, w_refs = refs[:n_lhs], refs[n_lhs:n_lhs + n_w]
    o_ref, wb_ref = refs[n_lhs + n_w:]
    m = pl.program_id(1)
    first = jnp.logical_or(m == 0, tg_ref[m] != tg_ref[jnp.maximum(m - 1, 0)])

    @pl.when(first)
    def _():
        for i in range(n_w):
            wb_ref[i] = w_refs[i][0].astype(BF16)

    for r0 in range(0, o_ref.shape[0], sub):
        live = rows_ref[m] > r0

        @pl.when(live)
        def _():
            pieces = [r[r0:r0 + sub, :] for r in lhs_refs]
            x = pieces[0] if n_lhs == 1 else jnp.concatenate(pieces, axis=-1)
            if swiglu:
                gate = jnp.dot(x, wb_ref[0], preferred_element_type=F32)
                up = jnp.dot(x, wb_ref[1], preferred_element_type=F32)
                out = _silu(gate) * up
            else:
                out = jnp.dot(x, wb_ref[0], preferred_element_type=F32)
            o_ref[r0:r0 + sub, :] = out.astype(o_ref.dtype)

        @pl.when(jnp.logical_not(live))
        def _():
            o_ref[r0:r0 + sub, :] = jnp.zeros((sub, o_ref.shape[1]), o_ref.dtype)


def _gmm(lhs, weights, tile_group, tile_rows, *, tm, tn, out_dtype, swiglu=False, sub=None, n_rows=None,
         name="gmm"):
    lhs = lhs if isinstance(lhs, (list, tuple)) else [lhs]
    n_rows = lhs[0].shape[0] if n_rows is None else n_rows
    _, k, nw = weights[0].shape
    assert sum(piece.shape[1] for piece in lhs) == k
    n_w = len(weights)
    grid_spec = pltpu.PrefetchScalarGridSpec(
        num_scalar_prefetch=2,
        grid=(nw // tn, n_rows // tm),
        in_specs=[pl.BlockSpec((tm, piece.shape[1]), lambda n, m, tg, tr: (m, 0)) for piece in lhs]
        + [pl.BlockSpec((1, k, tn), lambda n, m, tg, tr: (tg[m], 0, n)) for _ in range(n_w)],
        out_specs=pl.BlockSpec((tm, tn), lambda n, m, tg, tr: (m, n)),
        scratch_shapes=[pltpu.VMEM((n_w, k, tn), BF16)],
    )
    return pl.pallas_call(
        functools.partial(_gmm_kernel, n_lhs=len(lhs), n_w=n_w, swiglu=swiglu,
                          sub=tm if sub is None else sub),
        grid_spec=grid_spec,
        out_shape=jax.ShapeDtypeStruct((n_rows, nw), out_dtype),
        compiler_params=_params("arbitrary", "arbitrary"),
        name=name,
    )(tile_group, tile_rows, *lhs, *weights)


def _dense_matmul(lhs, w, *, tn, out_dtype, tiles=(1024, 768, 512, 256), swiglu=False, w2=None,
                  n_rows=None, name="matmul"):
    if n_rows is None:
        n_rows = (lhs[0] if isinstance(lhs, (list, tuple)) else lhs).shape[0]
    tm = _largest_tile(n_rows, tiles)
    n_tiles = n_rows // tm
    weights = [w[None]] if w2 is None else [w[None], w2[None]]
    return _gmm(lhs, weights, jnp.zeros((n_tiles,), jnp.int32), jnp.full((n_tiles,), tm, jnp.int32),
                tm=tm, tn=tn, out_dtype=out_dtype, swiglu=swiglu, n_rows=n_rows, name=name)


def _rope_tables(n_lat, n_ctx, head_dim, reps):
    n_rows_grid = n_lat // GRID_W
    rows = jnp.repeat(jnp.arange(n_rows_grid), GRID_W).astype(F32)
    cols = jnp.tile(jnp.arange(GRID_W), n_rows_grid).astype(F32)
    axis_dim = head_dim // 2
    freqs = ROPE_THETA ** (-jnp.arange(0, axis_dim, 2, dtype=F32) / axis_dim)
    ra, ca = rows[:, None] * freqs, cols[:, None] * freqs
    rc, rs, cc, cs = jnp.cos(ra), jnp.sin(ra), jnp.cos(ca), jnp.sin(ca)
    z = jnp.zeros_like(rs)
    cos = jnp.concatenate([rc, rc, cc, cc], axis=-1)
    sin_up = jnp.concatenate([-rs, z, -cs, z], axis=-1)
    sin_dn = jnp.concatenate([z, rs, z, cs], axis=-1)
    pad = lambda t, v: jnp.concatenate(
        [jnp.tile(t, (1, reps)), jnp.full((n_ctx, LANES), v, F32)], axis=0)
    return pad(cos, 1.0), pad(sin_up, 0.0), pad(sin_dn, 0.0)


def _rope(x, cos, sin_up, sin_dn, quarter):
    return (x * cos + pltpu.roll(x, LANES - quarter, 1) * sin_up
            + pltpu.roll(x, quarter, 1) * sin_dn)


def _head_rmsnorm(x, g):
    return x * lax.rsqrt(jnp.mean(x * x, axis=-1, keepdims=True) + NORM_EPS) * g


def _qkpost_kernel(p_ref, gq_ref, gk_ref, ca_ref, ua_ref, da_ref, cb_ref, ub_ref, db_ref,
                   q_ref, k_ref, u_ref, vt_ref):
    def col(c):
        return p_ref[:, c * LANES:(c + 1) * LANES]

    def put(ref, c, x):
        ref[:, c * LANES:(c + 1) * LANES] = x.astype(BF16)

    is_tail = pl.program_id(0) == pl.num_programs(0) - 1

    def put_t(c, x):
        xt = x.T
        vt_ref[0, c * LANES:(c + 1) * LANES, :] = jnp.where(is_tail, jnp.zeros_like(xt), xt).astype(BF16)

    rope_a = lambda x: _rope(x, ca_ref[...], ua_ref[...], da_ref[...], HEAD_DIM_A // 4)
    rope_b = lambda x: _rope(x, cb_ref[...], ub_ref[...], db_ref[...], HEAD_DIM_B // 4)
    scale_a, scale_b = LOG2_E * HEAD_DIM_A ** -0.5, LOG2_E * HEAD_DIM_B ** -0.5
    c0 = 0
    for h in range(N_HEADS_A):
        put(q_ref, h, rope_a(_head_rmsnorm(col(c0 + h), gq_ref[...])) * scale_a)
    c0 += N_HEADS_A
    for h in range(N_HEADS_B):
        put(q_ref, N_HEADS_A + h, rope_b(col(c0 + h)) * scale_b)
    c0 += N_HEADS_B
    for g in range(N_FOURIER_GROUPS):
        put(u_ref, g, col(c0 + g))
    c0 += N_FOURIER_GROUPS
    for h in range(N_KV_HEADS_A):
        put(k_ref, h, rope_a(_head_rmsnorm(col(c0 + h), gk_ref[...])))
    c0 += N_KV_HEADS_A
    for h in range(N_KV_HEADS_A):
        put_t(h, col(c0 + h))
    c0 += N_KV_HEADS_A
    for h in range(N_HEADS_B):
        put(k_ref, N_KV_HEADS_A + h, rope_b(col(c0 + h)))
    c0 += N_HEADS_B
    for h in range(N_HEADS_B):
        put_t(N_KV_HEADS_A + h, col(c0 + h))


Q_COLS = QA_COLS + QB_COLS
K_COLS = KA_COLS + KB_COLS
V_COLS = VA_COLS + VB_COLS


def _qkpost(p, g_q, g_k, rope_a, rope_b, n_lat):
    t = p.shape[0]
    n_tiles = t // ROW_TILE
    tiles_per_chunk = ATTN_TK // ROW_TILE
    assert tiles_per_chunk == 2 and (n_tiles + 1) % tiles_per_chunk == 0
    row = lambda w: pl.BlockSpec((ROW_TILE, w), lambda i: (jnp.minimum(i, n_tiles - 1), 0))
    vec = pl.BlockSpec((1, LANES), lambda i: (0, 0))
    vt_spec = pl.BlockSpec((1, V_COLS, ROW_TILE), lambda i: (i // tiles_per_chunk, 0, i % tiles_per_chunk))
    q, k, u, vt = pl.pallas_call(
        _qkpost_kernel,
        grid=(n_tiles + 1,),
        in_specs=[row(IN_COLS), vec, vec] + [row(LANES)] * 6,
        out_specs=[row(Q_COLS), row(K_COLS), row(C_COLS), vt_spec],
        out_shape=[jax.ShapeDtypeStruct((t, Q_COLS), BF16), jax.ShapeDtypeStruct((t, K_COLS), BF16),
                   jax.ShapeDtypeStruct((t, C_COLS), BF16),
                   jax.ShapeDtypeStruct((n_lat // ATTN_TK + 1, V_COLS, ATTN_TK), BF16)],
        compiler_params=_params("arbitrary"),
        name="qk_post",
    )(p, g_q.reshape(1, LANES), g_k.reshape(1, LANES), *rope_a, *rope_b)
    return q, k, u, vt


ATTN_TQ = 256
ATTN_TK = 512


REDUCE_ROWS = 32


def _stream_reduce(op, x):
    rows = x.shape[0]
    step = REDUCE_ROWS if rows % REDUCE_ROWS == 0 else rows
    acc = x[:step]
    for r in range(step, rows, step):
        acc = op(acc, x[r:r + step])
    red = jnp.max if op is jnp.maximum else jnp.sum
    return red(acc, axis=0, keepdims=True)


ATTN_TILES = 4


class _Tile:
    def __init__(self, refs, kv):
        self.qt, self.m, self.l, self.acc, s_a, c_a, s_b, c_b = refs
        self.stage = ((s_a, c_a), (s_b, c_b))
        self.kv = slice(kv * LANES, (kv + 1) * LANES)


def _attn_scratch():
    stat = pltpu.VMEM((1, ATTN_TQ), F32)
    score = pltpu.VMEM((ATTN_TK, ATTN_TQ), F32)
    per_tile = [pltpu.VMEM((LANES, ATTN_TQ), BF16), stat, stat, pltpu.VMEM((LANES, ATTN_TQ), F32),
                score, stat, score, stat]
    return per_tile * ATTN_TILES


def _make_tiles(scratch, kv_of_tile):
    n = len(scratch) // ATTN_TILES
    return [_Tile(scratch[n * i:n * (i + 1)], kv) for i, kv in enumerate(kv_of_tile)]


def _flash_t(tiles, k_ref, vt_ref, *, n_lat, n_ctx, is_lat):
    n_chunks = n_lat // ATTN_TK
    for t in tiles:
        t.m[...] = jnp.full(t.m.shape, NEG_BIG, F32)
        t.l[...] = jnp.zeros(t.l.shape, F32)
        t.acc[...] = jnp.zeros(t.acc.shape, F32)

    def scores(t, k, stage, rows):
        s = jnp.dot(k, t.qt[...], preferred_element_type=F32)
        s_ref, c_ref = t.stage[stage]
        s_ref[0:rows, :] = s
        c_ref[...] = _stream_reduce(jnp.maximum, s)

    def scores_lat(j, stage):
        off = pl.multiple_of(j * ATTN_TK, ATTN_TK)
        for t in tiles:
            scores(t, k_ref[pl.ds(off, ATTN_TK), t.kv], stage, ATTN_TK)

    def scores_ctx():
        for t in tiles:
            scores(t, k_ref[n_lat:n_lat + n_ctx, t.kv], 0, n_ctx)

    def consume(t, vt, stage, rows):
        s_ref, c_ref = t.stage[stage]
        m_old = t.m[...]
        m_new = jnp.maximum(m_old, c_ref[...])
        alpha = jnp.exp2(m_old - m_new)
        p = jnp.exp2(s_ref[0:rows, :] - m_new)
        t.l[...] = alpha * t.l[...] + _stream_reduce(jnp.add, p)
        t.acc[...] = alpha * t.acc[...] + jnp.dot(vt, p.astype(BF16), preferred_element_type=F32)
        t.m[...] = m_new

    def consume_lat(j, stage):
        for t in tiles:
            consume(t, vt_ref[j, t.kv, :], stage, ATTN_TK)

    @pl.when(is_lat)
    def _():
        scores_lat(0, 0)

        def body(i, carry):
            j = 2 * i
            scores_lat(j + 1, 1)
            consume_lat(j, 0)
            scores_lat(j + 2, 0)
            consume_lat(j + 1, 1)
            return carry
        lax.fori_loop(0, n_chunks // 2 - 1, body, 0)
        scores_lat(n_chunks - 1, 1)
        consume_lat(n_chunks - 2, 0)
        scores_ctx()
        consume_lat(n_chunks - 1, 1)

    @pl.when(jnp.logical_not(is_lat))
    def _():
        scores_ctx()

    for t in tiles:
        consume(t, vt_ref[n_chunks, t.kv, 0:n_ctx], 0, n_ctx)


def _kv_specs(t, n_chunks, width, first_blk):
    return [pl.BlockSpec((t, width), lambda g, i: (0, first_blk + g)),
            pl.BlockSpec((n_chunks + 1, width, ATTN_TK), lambda g, i: (0, first_blk + g, 0))]


def _gqa_kernel(q_ref, k_ref, vt_ref, o_ref, *scratch, n_lat, n_ctx):
    tiles = _make_tiles(scratch, [0] * ATTN_TILES)
    for h, t in enumerate(tiles):
        t.qt[...] = q_ref[:, h * LANES:(h + 1) * LANES].astype(F32).T.astype(BF16)
    is_lat = pl.program_id(1) < n_lat // ATTN_TQ
    _flash_t(tiles, k_ref, vt_ref, n_lat=n_lat, n_ctx=n_ctx, is_lat=is_lat)
    for h, t in enumerate(tiles):
        o_ref[:, h * LANES:(h + 1) * LANES] = (t.acc[...] / t.l[...]).T.astype(o_ref.dtype)


def _gqa_attention(q, k, vt, n_lat, n_ctx, n_q_rows):
    group = N_HEADS_A // N_KV_HEADS_A
    assert group == ATTN_TILES
    return pl.pallas_call(
        functools.partial(_gqa_kernel, n_lat=n_lat, n_ctx=n_ctx),
        grid=(N_KV_HEADS_A, n_q_rows // ATTN_TQ),
        in_specs=[pl.BlockSpec((ATTN_TQ, group * LANES), lambda g, i: (i, g))]
        + _kv_specs(n_lat + n_ctx, n_lat // ATTN_TK, LANES, 0),
        out_specs=pl.BlockSpec((ATTN_TQ, group * LANES), lambda g, i: (i, g)),
        out_shape=jax.ShapeDtypeStruct((n_q_rows, QA_COLS), BF16),
        scratch_shapes=_attn_scratch(),
        compiler_params=_params("arbitrary", "arbitrary"),
        name="gqa_attention",
    )(q, k, vt)


DIFF_HEADS_PER_STEP = 2


def _diff_kernel(q_ref, k_ref, vt_ref, lam_ref, gs_ref, o_ref, *scratch,
                 n_lat, n_ctx, lambda_init):
    tiles = _make_tiles(scratch, [0, 0, 1, 1])
    q = q_ref[...].astype(F32)
    lane = lax.broadcasted_iota(jnp.int32, (ATTN_TQ, LANES), 1)
    for h in range(DIFF_HEADS_PER_STEP):
        qh = q[:, h * LANES:(h + 1) * LANES]
        zero = jnp.zeros_like(qh)
        tiles[2 * h].qt[...] = jnp.where(lane < HEAD_DIM_B, qh, zero).T.astype(BF16)
        tiles[2 * h + 1].qt[...] = jnp.where(lane >= HEAD_DIM_B, qh, zero).T.astype(BF16)
    is_lat = pl.program_id(1) < n_lat // ATTN_TQ
    _flash_t(tiles, k_ref, vt_ref, n_lat=n_lat, n_ctx=n_ctx, is_lat=is_lat)
    lv = lam_ref[...]
    lam = (jnp.exp(jnp.sum(lv[0:1] * lv[1:2], axis=-1, keepdims=True))
           - jnp.exp(jnp.sum(lv[2:3] * lv[3:4], axis=-1, keepdims=True)) + lambda_init)
    for h in range(DIFF_HEADS_PER_STEP):
        out1, out2 = (t.acc[...] / t.l[...] for t in tiles[2 * h:2 * h + 2])
        diff = (out1 - lam * out2).T
        o_ref[:, h * LANES:(h + 1) * LANES] = (
            _head_rmsnorm(diff, gs_ref[...]) * (1.0 - lambda_init)).astype(o_ref.dtype)


def _diff_attention(q, k, vt, lam_vecs, g_subln, lambda_init, n_lat, n_ctx, n_q_rows):
    width = DIFF_HEADS_PER_STEP * LANES
    assert 2 * DIFF_HEADS_PER_STEP == ATTN_TILES and QA_COLS % width == 0 and KA_COLS % width == 0
    return pl.pallas_call(
        functools.partial(_diff_kernel, n_lat=n_lat, n_ctx=n_ctx, lambda_init=lambda_init),
        grid=(N_HEADS_B // DIFF_HEADS_PER_STEP, n_q_rows // ATTN_TQ),
        in_specs=[pl.BlockSpec((ATTN_TQ, width), lambda h, i: (i, QA_COLS // width + h))]
        + _kv_specs(n_lat + n_ctx, n_lat // ATTN_TK, width, KA_COLS // width)
        + [pl.BlockSpec((4, HEAD_DIM_B), lambda h, i: (0, 0)),
           pl.BlockSpec((1, LANES), lambda h, i: (0, 0))],
        out_specs=pl.BlockSpec((ATTN_TQ, width), lambda h, i: (i, h)),
        out_shape=jax.ShapeDtypeStruct((n_q_rows, VB_COLS), BF16),
        scratch_shapes=_attn_scratch(),
        compiler_params=_params("arbitrary", "arbitrary"),
        name="diff_attention",
    )(q, k, vt, lam_vecs, g_subln.reshape(1, LANES))


def _dft_cos_sin(n):
    j = np.arange(n, dtype=np.float64)
    ang = 2.0 * np.pi * np.outer(j, j) / n
    return np.cos(ang), np.sin(ang)


def _channel_mix_weights(cs_ref, wf_ref, ab_ref):
    cs = cs_ref[...].astype(BF16)
    for g in range(N_FOURIER_GROUPS):
        ab_ref[g] = jnp.dot(cs, wf_ref[g].astype(BF16), preferred_element_type=F32).astype(BF16)


def _channel_mix(xr, xi, ab_ref, norm):
    outs = []
    for g in range(N_FOURIER_GROUPS):
        sl = slice(g * LANES, (g + 1) * LANES)
        lhs = jnp.concatenate([xr[:, sl], xi[:, sl]], axis=-1).astype(BF16)
        outs.append(jnp.dot(lhs, ab_ref[g], preferred_element_type=F32))
    return jnp.concatenate(outs, axis=-1) * norm


def _fourier_stage1_kernel(w_ref, u_ref, o_ref):
    o_ref[...] = jnp.dot(w_ref[...].astype(BF16), u_ref[...],
                         preferred_element_type=F32).astype(o_ref.dtype)


def _fourier_stage2_kernel(a_ref, tc_ref, ts_ref, w3_ref, cs_ref, wf_ref, o_ref, ab_ref, *, norm):
    @pl.when(pl.program_id(0) == 0)
    def _():
        _channel_mix_weights(cs_ref, wf_ref, ab_ref)

    ar = a_ref[0, 0].astype(F32)
    ai = a_ref[1, 0].astype(F32)
    tc, ts = tc_ref[0], ts_ref[0]
    br = ar * tc + ai * ts
    bi = ai * tc - ar * ts
    stack = jnp.concatenate([br, bi], axis=0).astype(BF16)
    x = jnp.dot(w3_ref[...].astype(BF16), stack, preferred_element_type=F32)
    half = x.shape[0] // 2
    o_ref[...] = _channel_mix(x[:half], x[half:], ab_ref, norm).astype(o_ref.dtype)


def _fourier_ctx_kernel(u_ref, w_ref, cs_ref, wf_ref, o_ref, ab_ref, *, norm):
    _channel_mix_weights(cs_ref, wf_ref, ab_ref)
    x = jnp.dot(w_ref[...].astype(BF16), u_ref[...], preferred_element_type=F32)
    half = x.shape[0] // 2
    o_ref[...] = _channel_mix(x[:half], x[half:], ab_ref, norm).astype(o_ref.dtype)


def _fourier_latent(u, n, w_fourier):
    assert u.shape[0] % LANES == 0
    n2 = LANES
    n1 = n // n2
    c1, s1 = _dft_cos_sin(n1)
    w1 = jnp.asarray(np.concatenate([c1, -s1], axis=0), F32)
    k1n2 = 2.0 * np.pi * np.outer(np.arange(n1), np.arange(n2)) / n
    tw_c = jnp.asarray(np.cos(k1n2)[:, :, None], F32)
    tw_s = jnp.asarray(np.sin(k1n2)[:, :, None], F32)
    c2, s2 = _dft_cos_sin(n2)
    w3 = jnp.asarray(np.block([[c2, s2], [-s2, c2]]), F32)
    cc, sc = _dft_cos_sin(FOURIER_GROUP_DIM)
    cs = jnp.asarray(np.concatenate([cc, sc], axis=0), F32)
    width = n2 * C_COLS
    cb = _largest_tile(width, (4096, 2048, 1024, 512))
    stage1 = pl.pallas_call(
        _fourier_stage1_kernel,
        grid=(width // cb,),
        in_specs=[pl.BlockSpec((2 * n1, n1), lambda j: (0, 0)),
                  pl.BlockSpec((n1, cb), lambda j: (0, j))],
        out_specs=pl.BlockSpec((2 * n1, cb), lambda j: (0, j)),
        out_shape=jax.ShapeDtypeStruct((2 * n1, width), BF16),
        compiler_params=_params("arbitrary"),
        name="fourier_stage1",
    )(w1, u.reshape(u.shape[0] // n2, width))
    a = stage1.reshape(2, n1, n2, C_COLS)
    norm = 1.0 / math.sqrt(n * FOURIER_GROUP_DIM)
    out = pl.pallas_call(
        functools.partial(_fourier_stage2_kernel, norm=norm),
        grid=(n1,),
        in_specs=[pl.BlockSpec((2, 1, n2, C_COLS), lambda k: (0, k, 0, 0)),
                  pl.BlockSpec((1, n2, 1), lambda k: (k, 0, 0)),
                  pl.BlockSpec((1, n2, 1), lambda k: (k, 0, 0)),
                  pl.BlockSpec((2 * n2, 2 * n2), lambda k: (0, 0)),
                  pl.BlockSpec((2 * FOURIER_GROUP_DIM, FOURIER_GROUP_DIM), lambda k: (0, 0)),
                  pl.BlockSpec((N_FOURIER_GROUPS, FOURIER_GROUP_DIM, FOURIER_GROUP_DIM),
                               lambda k: (0, 0, 0))],
        out_specs=pl.BlockSpec((n2, C_COLS), lambda k: (0, k)),
        out_shape=jax.ShapeDtypeStruct((n2, n1 * C_COLS), BF16),
        scratch_shapes=[pltpu.VMEM((N_FOURIER_GROUPS, 2 * FOURIER_GROUP_DIM, FOURIER_GROUP_DIM), BF16)],
        compiler_params=_params("arbitrary"),
        name="fourier_stage2",
    )(a, tw_c, tw_s, w3, cs, w_fourier)
    return out.reshape(n, C_COLS)


def _fourier_ctx(u_ctx, w_fourier):
    m = u_ctx.shape[0]
    c, s = _dft_cos_sin(m)
    w = jnp.asarray(np.concatenate([c, -s], axis=0), F32)
    cc, sc = _dft_cos_sin(FOURIER_GROUP_DIM)
    cs = jnp.asarray(np.concatenate([cc, sc], axis=0), F32)
    norm = 1.0 / math.sqrt(m * FOURIER_GROUP_DIM)
    return pl.pallas_call(
        functools.partial(_fourier_ctx_kernel, norm=norm),
        out_shape=jax.ShapeDtypeStruct((m, C_COLS), BF16),
        scratch_shapes=[pltpu.VMEM((N_FOURIER_GROUPS, 2 * FOURIER_GROUP_DIM, FOURIER_GROUP_DIM), BF16)],
        compiler_params=pltpu.CompilerParams(vmem_limit_bytes=VMEM_LIMIT),
        name="fourier_ctx",
    )(u_ctx, w, cs, w_fourier)


def _gated_residual(x, f, g, mod, gate_row):
    ms = jnp.mean(f * f, axis=-1, keepdims=True)
    return x + mod[gate_row:gate_row + 1] * (f * lax.rsqrt(ms + NORM_EPS) * g)


def _residual_kernel(x_ref, f_ref, g_ref, mod_ref, o_ref, *, gate_row):
    o_ref[...] = _gated_residual(x_ref[...], f_ref[...], g_ref[...], mod_ref[0], gate_row)


def _post_residual(x, f, g, mod, gate_row, n_rows, n_lat):
    d = x.shape[1]
    row = pl.BlockSpec((ROW_TILE, d), lambda i: (i, 0))
    return pl.pallas_call(
        functools.partial(_residual_kernel, gate_row=gate_row),
        grid=(n_rows // ROW_TILE,),
        in_specs=[row, row, pl.BlockSpec((1, d), lambda i: (0, 0)), _mod_spec(d, n_lat // ROW_TILE)],
        out_specs=row,
        out_shape=jax.ShapeDtypeStruct((n_rows, d), F32),
        compiler_params=_params("arbitrary"),
        name="post_residual",
    )(x, f, g.reshape(1, d), mod)


MOE_TM = 512
MOE_SUB = 256
R_IDX, R_W, R_RANK = 0, 2, 4


def _router_kernel(x_ref, g_ref, mod_ref, wr_ref, h_ref, route_ref, cnt_ref, carry_ref, *, shift_row):
    @pl.when(pl.program_id(0) == 0)
    def _():
        carry_ref[...] = jnp.zeros(carry_ref.shape, F32)

    h = _norm_modulate(x_ref[...], g_ref[...], mod_ref[0], shift_row)
    h_ref[...] = h
    logits = jnp.dot(h, wr_ref[...], preferred_element_type=F32, precision=lax.Precision.HIGHEST)
    rows = logits.shape[0]
    lane = lax.broadcasted_iota(jnp.int32, logits.shape, 1).astype(F32)
    neg = jnp.full(logits.shape, -jnp.inf, F32)
    l1 = jnp.where(lane < N_EXPERTS, logits, neg)
    m1 = jnp.max(l1, axis=-1, keepdims=True)
    i1 = jnp.min(jnp.where(l1 == m1, lane, float(LANES)), axis=-1, keepdims=True)
    l2 = jnp.where(lane == i1, neg, l1)
    m2 = jnp.max(l2, axis=-1, keepdims=True)
    i2 = jnp.min(jnp.where(l2 == m2, lane, float(LANES)), axis=-1, keepdims=True)
    e2 = jnp.exp(m2 - m1)
    w1 = 1.0 / (1.0 + e2)
    w2 = e2 / (1.0 + e2)
    hit1 = lane == i1
    hit2 = lane == i2
    onehot = jnp.logical_or(hit1, hit2).astype(BF16)
    r_i = lax.broadcasted_iota(jnp.int32, (rows, rows), 0)
    c_i = lax.broadcasted_iota(jnp.int32, (rows, rows), 1)
    before = (c_i < r_i).astype(BF16)
    base = carry_ref[...] + jnp.dot(before, onehot, preferred_element_type=F32)
    rank1 = jnp.sum(jnp.where(hit1, base, 0.0), axis=-1, keepdims=True)
    rank2 = jnp.sum(jnp.where(hit2, base, 0.0), axis=-1, keepdims=True)
    carry_ref[...] = carry_ref[...] + jnp.sum(onehot.astype(F32), axis=0, keepdims=True)
    rec = jnp.zeros(logits.shape, F32)
    for off, val in ((R_IDX, i1), (R_IDX + 1, i2), (R_W, w1), (R_W + 1, w2),
                     (R_RANK, rank1), (R_RANK + 1, rank2)):
        rec = jnp.where(lane == off, val, rec)
    route_ref[...] = rec
    cnt_ref[...] = jnp.broadcast_to(carry_ref[...], cnt_ref.shape)


def _router(x, g, mod, w_router, n_lat):
    d = x.shape[1]
    wr = jnp.zeros((d, LANES), F32).at[:, :N_EXPERTS].set(w_router)
    return pl.pallas_call(
        functools.partial(_router_kernel, shift_row=3),
        grid=(n_lat // ROW_TILE,),
        in_specs=[pl.BlockSpec((ROW_TILE, d), lambda i: (i, 0)),
                  pl.BlockSpec((1, d), lambda i: (0, 0)),
                  pl.BlockSpec((1, N_MOD, d), lambda i: (0, 0, 0)),
                  pl.BlockSpec((d, LANES), lambda i: (0, 0))],
        out_specs=[pl.BlockSpec((ROW_TILE, d), lambda i: (i, 0)),
                   pl.BlockSpec((ROW_TILE, LANES), lambda i: (i, 0)),
                   pl.BlockSpec((8, LANES), lambda i: (0, 0))],
        out_shape=[jax.ShapeDtypeStruct((n_lat, d), F32),
                   jax.ShapeDtypeStruct((n_lat, LANES), F32),
                   jax.ShapeDtypeStruct((8, LANES), F32)],
        scratch_shapes=[pltpu.VMEM((1, LANES), F32)],
        compiler_params=_params("arbitrary"),
        name="router",
    )(x, g.reshape(1, d), mod, wr)


def _row_copy(src_hbm, src_row, dst_ref, dst_row, sem):
    return pltpu.make_async_copy(src_hbm.at[pl.ds(src_row, 1)], dst_ref.at[pl.ds(dst_row, 1)], sem)


def _wait_rows(src_hbm, dst_ref, sem):
    pltpu.make_async_copy(src_hbm.at[pl.ds(0, dst_ref.shape[0])], dst_ref, sem).wait()


ISSUE_UNROLL = 8


def _dispatch_kernel(src_ref, nv_ref, h_hbm, o_ref, buf_ref, sems):
    i = pl.program_id(0)
    n_valid = nv_ref[0]
    slot = lax.rem(i, 2)

    def issue(tile, slot):
        base = tile * MOE_TM

        def body(r, carry):
            _row_copy(h_hbm, src_ref[base + r], buf_ref.at[slot], r, sems.at[slot]).start()
            return carry
        lax.fori_loop(0, MOE_TM, body, 0, unroll=ISSUE_UNROLL)

    @pl.when(jnp.logical_and(i == 0, n_valid > 0))
    def _():
        issue(0, 0)

    @pl.when(i + 1 < n_valid)
    def _():
        issue(i + 1, 1 - slot)

    @pl.when(i < n_valid)
    def _():
        _wait_rows(h_hbm, buf_ref.at[slot], sems.at[slot])
        o_ref[...] = buf_ref[slot].astype(o_ref.dtype)

    @pl.when(i >= n_valid)
    def _():
        o_ref[...] = jnp.zeros(o_ref.shape, o_ref.dtype)


def _dispatch(src_token, n_valid, h):
    d = h.shape[1]
    n_slots = src_token.shape[0]
    grid_spec = pltpu.PrefetchScalarGridSpec(
        num_scalar_prefetch=2,
        grid=(n_slots // MOE_TM,),
        in_specs=[pl.BlockSpec(memory_space=pl.ANY)],
        out_specs=pl.BlockSpec((MOE_TM, d), lambda i, src, nv: (i, 0)),
        scratch_shapes=[pltpu.VMEM((2, MOE_TM, d), F32), pltpu.SemaphoreType.DMA((2,))],
    )
    return pl.pallas_call(
        _dispatch_kernel,
        grid_spec=grid_spec,
        out_shape=jax.ShapeDtypeStruct((n_slots, d), BF16),
        compiler_params=_params("arbitrary"),
        name="moe_dispatch",
    )(src_token, n_valid, h)


def _combine_kernel(dest_ref, y_hbm, route_ref, x_ref, g_ref, mod_ref, o_ref, buf_ref, sems, *, gate_row):
    i = pl.program_id(0)
    slot = lax.rem(i, 2)

    def issue(tile, slot):
        base = tile * ROW_TILE

        def body(r, carry):
            for k in range(TOP_K):
                _row_copy(y_hbm, dest_ref[TOP_K * (base + r) + k], buf_ref.at[slot], k * ROW_TILE + r,
                          sems.at[slot]).start()
            return carry
        lax.fori_loop(0, ROW_TILE, body, 0, unroll=ISSUE_UNROLL)

    @pl.when(i == 0)
    def _():
        issue(0, 0)

    @pl.when(i + 1 < pl.num_programs(0))
    def _():
        issue(i + 1, 1 - slot)

    _wait_rows(y_hbm, buf_ref.at[slot], sems.at[slot])
    route = route_ref[...]
    f = (route[:, R_W:R_W + 1] * buf_ref[slot, 0:ROW_TILE, :]
         + route[:, R_W + 1:R_W + 2] * buf_ref[slot, ROW_TILE:TOP_K * ROW_TILE, :])
    o_ref[...] = _gated_residual(x_ref[...], f, g_ref[...], mod_ref[0], gate_row)


def _combine_residual(dest_flat, y_sorted, route, x, g, mod, gate_row, n_lat):
    d = x.shape[1]
    row = lambda w: pl.BlockSpec((ROW_TILE, w), lambda i, dest: (i, 0))
    grid_spec = pltpu.PrefetchScalarGridSpec(
        num_scalar_prefetch=1,
        grid=(n_lat // ROW_TILE,),
        in_specs=[pl.BlockSpec(memory_space=pl.ANY), row(LANES), row(d),
                  pl.BlockSpec((1, d), lambda i, dest: (0, 0)),
                  pl.BlockSpec((1, N_MOD, d), lambda i, dest: (0, 0, 0))],
        out_specs=row(d),
        scratch_shapes=[pltpu.VMEM((2, TOP_K * ROW_TILE, d), F32), pltpu.SemaphoreType.DMA((2,))],
    )
    return pl.pallas_call(
        functools.partial(_combine_kernel, gate_row=gate_row),
        grid_spec=grid_spec,
        out_shape=jax.ShapeDtypeStruct((n_lat, d), F32),
        compiler_params=_params("arbitrary"),
        name="moe_combine",
    )(dest_flat, y_sorted, route, x, g.reshape(1, d), mod)


def _tile_meta(starts, counts, ends, n_slots, tm):
    tile_start = jnp.arange(n_slots // tm, dtype=jnp.int32) * tm
    group = jnp.minimum(jnp.sum((tile_start[:, None] >= ends[None, :]).astype(jnp.int32), axis=1),
                        N_EXPERTS - 1)
    rows = jnp.clip(starts[group] + counts[group] - tile_start, 0, tm)
    return group, rows


def _moe_layer(x, g_pre, g_post, mod, w_router, w_gate, w_up, w_down, n_lat):
    d_ff = w_gate.shape[-1]
    h, route, cnt = _router(x, g_pre, mod, w_router, n_lat)
    experts = route[:, R_IDX:R_IDX + TOP_K].astype(jnp.int32)
    ranks = route[:, R_RANK:R_RANK + TOP_K].astype(jnp.int32)
    counts = cnt[0, :N_EXPERTS].astype(jnp.int32)
    padded = ((counts + MOE_TM - 1) // MOE_TM) * MOE_TM
    ends = jnp.cumsum(padded)
    starts = ends - padded
    dest = (starts[experts] + ranks).reshape(-1)
    n_slots = TOP_K * n_lat + N_EXPERTS * MOE_TM
    n_valid = (ends[-1:] // MOE_TM).astype(jnp.int32)

    token = jnp.repeat(jnp.arange(n_lat, dtype=jnp.int32), TOP_K)
    src_token = jnp.zeros((n_slots,), jnp.int32).at[dest].set(token)
    hs = _dispatch(src_token, n_valid, h)
    group, rows = _tile_meta(starts, counts, ends, n_slots, MOE_TM)
    act = _gmm(hs, [w_gate, w_up], group, rows, tm=MOE_TM, sub=MOE_SUB,
               tn=_largest_tile(d_ff, (512, 256, 128)), out_dtype=BF16, swiglu=True, name="moe_gate_up")
    group, rows = _tile_meta(starts, counts, ends, n_slots, MOE_SUB)
    y = _gmm(act, [w_down], group, rows, tm=MOE_SUB,
             tn=_largest_tile(x.shape[1], (512, 256, 128)), out_dtype=F32, name="moe_down")
    return _combine_residual(dest, y, route, x, g_post, mod, 5, n_lat)


def kernel(x, c, ctx, c_ctx, w_mod, b_mod, g_pre_mix, g_post_mix, g_pre_ffn, g_post_ffn, w_in, g_q_a, g_k_a, lambda_q1, lambda_k1, lambda_q2, lambda_k2, g_subln_b, w_fourier, w_out, w_gate_dense, w_up_dense, w_down_dense, w_router, w_gate_moe, w_up_moe, w_down_moe):
    batch, n_lat, d = x.shape
    n_ctx = ctx.shape[1]
    depth = w_mod.shape[0]
    assert batch == 1 and n_ctx == ROW_TILE and n_lat % max(2 * ATTN_TK, MOE_TM, GRID_W) == 0
    assert depth == 2, "layer 0 dense, layer 1 (last) routed experts"
    t = n_lat + n_ctx

    cond_rows = jnp.concatenate([c, c_ctx[None, :], jnp.zeros((6, d), F32)], axis=0)
    mod_all = _modulation(cond_rows, w_mod, b_mod)
    rope_a = _rope_tables(n_lat, n_ctx, HEAD_DIM_A, 1)
    rope_b = _rope_tables(n_lat, n_ctx, HEAD_DIM_B, 2)

    xs = jnp.concatenate([x[0], ctx[0]], axis=0)
    for layer in range(depth):
        last = layer == depth - 1
        n_out = n_lat if last else t
        mod = mod_all[layer, :2].reshape(2, N_MOD, d)
        lambda_init = 0.8 - 0.6 * math.exp(-0.3 * layer)

        h = _prenorm(xs, g_pre_mix[layer], mod, 0, t, n_lat)
        p = _dense_matmul(h, w_in[layer], tn=512, out_dtype=F32, name="in_proj")
        q, k, u, vt = _qkpost(p, g_q_a[layer], g_k_a[layer], rope_a, rope_b, n_lat)
        y_a = _gqa_attention(q, k, vt, n_lat, n_ctx, n_out)
        lam_vecs = jnp.stack([lambda_q1[layer], lambda_k1[layer], lambda_q2[layer], lambda_k2[layer]])
        y_b = _diff_attention(q, k, vt, lam_vecs, g_subln_b[layer], lambda_init,
                              n_lat, n_ctx, n_out)
        y_c = _fourier_latent(u, n_lat, w_fourier[layer])
        if not last:
            y_c = jnp.concatenate([y_c, _fourier_ctx(u[n_lat:], w_fourier[layer])], axis=0)
        f = _dense_matmul([y_a, y_b, y_c], w_out[layer], tn=512, out_dtype=F32, name="out_proj")
        xs = _post_residual(xs, f, g_post_mix[layer], mod, 2, n_out, n_lat)

        j = layer // 2
        if layer % 2 == 0:
            h = _prenorm(xs, g_pre_ffn[layer], mod, 3, n_out, n_lat)
            act = _dense_matmul(h, w_gate_dense[j], w2=w_up_dense[j], swiglu=True, tn=512,
                                out_dtype=BF16, name="ffn_gate_up")
            f = _dense_matmul(act, w_down_dense[j], tn=512, tiles=(MOE_SUB,), out_dtype=F32, name="ffn_down")
            xs = _post_residual(xs, f, g_post_ffn[layer], mod, 5, n_out, n_lat)
        else:
            xs = _moe_layer(xs, g_pre_ffn[layer], g_post_ffn[layer], mod, w_router[j],
                            w_gate_moe[j], w_up_moe[j], w_down_moe[j], n_lat)
    return xs[:n_lat][None]
```

```python
import functools
import math

import numpy as np
import jax
import jax.numpy as jnp
from jax import lax
from jax.experimental import pallas as pl
from jax.experimental.pallas import tpu as pltpu

F32 = jnp.float32
BF16 = jnp.bfloat16

GRID_W = 64
ROPE_THETA = 10000.0
NORM_EPS = 1e-6
N_HEADS_A, N_KV_HEADS_A, HEAD_DIM_A = 8, 2, 128
N_HEADS_B, HEAD_DIM_B = 4, 64
VALUE_DIM_B = 2 * HEAD_DIM_B
N_FOURIER_GROUPS, FOURIER_GROUP_DIM = 4, 128
QA_COLS = N_HEADS_A * HEAD_DIM_A
QB_COLS = N_HEADS_B * 2 * HEAD_DIM_B
C_COLS = N_FOURIER_GROUPS * FOURIER_GROUP_DIM
KA_COLS = N_KV_HEADS_A * HEAD_DIM_A
VA_COLS = KA_COLS
KB_COLS = QB_COLS
VB_COLS = N_HEADS_B * VALUE_DIM_B
KV_START = QA_COLS + QB_COLS + C_COLS
IN_COLS = KV_START + KA_COLS + VA_COLS + KB_COLS + VB_COLS
N_EXPERTS, TOP_K = 8, 2
N_MOD = 6

LANES = 128
V7X_VMEM_BYTES = 64 * 1024 * 1024
VMEM_LIMIT = 52 * 1024 * 1024

ROW_TILE = 256
NEG_BIG = -1e30
LOG2_E = math.log2(math.e)


def _params(*sem):
    return pltpu.CompilerParams(dimension_semantics=sem, vmem_limit_bytes=VMEM_LIMIT)


def _silu(x):
    return x / (1.0 + jnp.exp(-x))


def _largest_tile(n, candidates):
    for c in candidates:
        if n % c == 0:
            return c
    raise ValueError(f"no tile in {candidates} divides {n}")


def _mod_kernel(c_ref, w_ref, b_ref, o_ref):
    cond = _silu(c_ref[...])
    o_ref[0] = jnp.dot(cond.astype(BF16), w_ref[0].astype(BF16),
                       preferred_element_type=F32) + b_ref[0]


def _modulation(cond_rows, w_mod, b_mod):
    depth, d, n6 = w_mod.shape
    tn = _largest_tile(n6, (1024, 512, 256, 128))
    return pl.pallas_call(
        _mod_kernel,
        grid=(depth, n6 // tn),
        in_specs=[pl.BlockSpec((8, d), lambda l, j: (0, 0)),
                  pl.BlockSpec((1, d, tn), lambda l, j: (l, 0, j)),
                  pl.BlockSpec((1, 1, tn), lambda l, j: (l, 0, j))],
        out_specs=pl.BlockSpec((1, 8, tn), lambda l, j: (l, 0, j)),
        out_shape=jax.ShapeDtypeStruct((depth, 8, n6), F32),
        compiler_params=_params("arbitrary", "arbitrary"),
        name="modulation",
    )(cond_rows, w_mod, b_mod.reshape(depth, 1, n6))


def _norm_modulate(x, g, mod, shift_row):
    ms = jnp.mean(x * x, axis=-1, keepdims=True)
    y = x * lax.rsqrt(ms + NORM_EPS) * g
    return y * (1.0 + mod[shift_row + 1:shift_row + 2]) + mod[shift_row:shift_row + 1]


def _prenorm_kernel(x_ref, g_ref, mod_ref, o_ref, *, shift_row):
    h = _norm_modulate(x_ref[...], g_ref[...], mod_ref[0], shift_row)
    o_ref[...] = h.astype(o_ref.dtype)


def _mod_spec(d, n_lat_tiles):
    return pl.BlockSpec((1, N_MOD, d), lambda i: ((i >= n_lat_tiles).astype(jnp.int32), 0, 0))


def _prenorm(x, g, mod, shift_row, n_rows, n_lat):
    d = x.shape[1]
    return pl.pallas_call(
        functools.partial(_prenorm_kernel, shift_row=shift_row),
        grid=(n_rows // ROW_TILE,),
        in_specs=[pl.BlockSpec((ROW_TILE, d), lambda i: (i, 0)),
                  pl.BlockSpec((1, d), lambda i: (0, 0)),
                  _mod_spec(d, n_lat // ROW_TILE)],
        out_specs=pl.BlockSpec((ROW_TILE, d), lambda i: (i, 0)),
        out_shape=jax.ShapeDtypeStruct((n_rows, d), BF16),
        compiler_params=_params("arbitrary"),
        name="prenorm",
    )(x, g.reshape(1, d), mod)


def _gmm_kernel(tg_ref, rows_ref, *refs, n_lhs, n_w, swiglu):
    lhs_refs, w_refs = refs[:n_lhs], refs[n_lhs:n_lhs + n_w]
    o_ref, wb_ref = refs[n_lhs + n_w:]
    m = pl.program_id(1)
    first = jnp.logical_or(m == 0, tg_ref[m] != tg_ref[jnp.maximum(m - 1, 0)])

    @pl.when(first)
    def _():
        for i in range(n_w):
            wb_ref[i] = w_refs[i][0].astype(BF16)

    live = rows_ref[m] > 0

    @pl.when(live)
    def _():
        pieces = [r[...] for r in lhs_refs]
        x = pieces[0] if n_lhs == 1 else jnp.concatenate(pieces, axis=-1)
        if swiglu:
            gate = jnp.dot(x, wb_ref[0], preferred_element_type=F32)
            up = jnp.dot(x, wb_ref[1], preferred_element_type=F32)
            out = _silu(gate) * up
        else:
            out = jnp.dot(x, wb_ref[0], preferred_element_type=F32)
        o_ref[...] = out.astype(o_ref.dtype)

    @pl.when(jnp.logical_not(live))
    def _():
        o_ref[...] = jnp.zeros(o_ref.shape, o_ref.dtype)


def _gmm(lhs, weights, tile_group, tile_rows, *, tm, tn, out_dtype, swiglu=False, n_rows=None, name="gmm"):
    lhs = lhs if isinstance(lhs, (list, tuple)) else [lhs]
    n_rows = lhs[0].shape[0] if n_rows is None else n_rows
    _, k, nw = weights[0].shape
    assert sum(piece.shape[1] for piece in lhs) == k
    n_w = len(weights)
    grid_spec = pltpu.PrefetchScalarGridSpec(
        num_scalar_prefetch=2,
        grid=(nw // tn, n_rows // tm),
        in_specs=[pl.BlockSpec((tm, piece.shape[1]), lambda n, m, tg, tr: (m, 0)) for piece in lhs]
        + [pl.BlockSpec((1, k, tn), lambda n, m, tg, tr: (tg[m], 0, n)) for _ in range(n_w)],
        out_specs=pl.BlockSpec((tm, tn), lambda n, m, tg, tr: (m, n)),
        scratch_shapes=[pltpu.VMEM((n_w, k, tn), BF16)],
    )
    return pl.pallas_call(
        functools.partial(_gmm_kernel, n_lhs=len(lhs), n_w=n_w, swiglu=swiglu),
        grid_spec=grid_spec,
        out_shape=jax.ShapeDtypeStruct((n_rows, nw), out_dtype),
        compiler_params=_params("arbitrary", "arbitrary"),
        name=name,
    )(tile_group, tile_rows, *lhs, *weights)


def _dense_matmul(lhs, w, *, tn, out_dtype, tiles=(1024, 768, 512, 256), swiglu=False, w2=None,
                  n_rows=None, name="matmul"):
    if n_rows is None:
        n_rows = (lhs[0] if isinstance(lhs, (list, tuple)) else lhs).shape[0]
    tm = _largest_tile(n_rows, tiles)
    n_tiles = n_rows // tm
    weights = [w[None]] if w2 is None else [w[None], w2[None]]
    return _gmm(lhs, weights, jnp.zeros((n_tiles,), jnp.int32), jnp.full((n_tiles,), tm, jnp.int32),
                tm=tm, tn=tn, out_dtype=out_dtype, swiglu=swiglu, n_rows=n_rows, name=name)


def _rope_tables(n_lat, n_ctx, head_dim, reps):
    n_rows_grid = n_lat // GRID_W
    rows = jnp.repeat(jnp.arange(n_rows_grid), GRID_W).astype(F32)
    cols = jnp.tile(jnp.arange(GRID_W), n_rows_grid).astype(F32)
    axis_dim = head_dim // 2
    freqs = ROPE_THETA ** (-jnp.arange(0, axis_dim, 2, dtype=F32) / axis_dim)
    ra, ca = rows[:, None] * freqs, cols[:, None] * freqs
    rc, rs, cc, cs = jnp.cos(ra), jnp.sin(ra), jnp.cos(ca), jnp.sin(ca)
    z = jnp.zeros_like(rs)
    cos = jnp.concatenate([rc, rc, cc, cc], axis=-1)
    sin_up = jnp.concatenate([-rs, z, -cs, z], axis=-1)
    sin_dn = jnp.concatenate([z, rs, z, cs], axis=-1)
    pad = lambda t, v: jnp.concatenate(
        [jnp.tile(t, (1, reps)), jnp.full((n_ctx, LANES), v, F32)], axis=0)
    return pad(cos, 1.0), pad(sin_up, 0.0), pad(sin_dn, 0.0)


def _rope(x, cos, sin_up, sin_dn, quarter):
    return (x * cos + pltpu.roll(x, LANES - quarter, 1) * sin_up
            + pltpu.roll(x, quarter, 1) * sin_dn)


def _head_rmsnorm(x, g):
    return x * lax.rsqrt(jnp.mean(x * x, axis=-1, keepdims=True) + NORM_EPS) * g


def _qkpost_kernel(p_ref, gq_ref, gk_ref, ca_ref, ua_ref, da_ref, cb_ref, ub_ref, db_ref,
                   q_ref, k_ref, u_ref, vt_ref):
    def col(c):
        return p_ref[:, c * LANES:(c + 1) * LANES]

    def put(ref, c, x):
        ref[:, c * LANES:(c + 1) * LANES] = x.astype(BF16)

    is_tail = pl.program_id(0) == pl.num_programs(0) - 1

    def put_t(c, x):
        xt = x.T
        vt_ref[0, c * LANES:(c + 1) * LANES, :] = jnp.where(is_tail, jnp.zeros_like(xt), xt).astype(BF16)

    rope_a = lambda x: _rope(x, ca_ref[...], ua_ref[...], da_ref[...], HEAD_DIM_A // 4)
    rope_b = lambda x: _rope(x, cb_ref[...], ub_ref[...], db_ref[...], HEAD_DIM_B // 4)
    scale_a, scale_b = LOG2_E * HEAD_DIM_A ** -0.5, LOG2_E * HEAD_DIM_B ** -0.5
    c0 = 0
    for h in range(N_HEADS_A):
        put(q_ref, h, rope_a(_head_rmsnorm(col(c0 + h), gq_ref[...])) * scale_a)
    c0 += N_HEADS_A
    for h in range(N_HEADS_B):
        put(q_ref, N_HEADS_A + h, rope_b(col(c0 + h)) * scale_b)
    c0 += N_HEADS_B
    for g in range(N_FOURIER_GROUPS):
        put(u_ref, g, col(c0 + g))
    c0 += N_FOURIER_GROUPS
    for h in range(N_KV_HEADS_A):
        put(k_ref, h, rope_a(_head_rmsnorm(col(c0 + h), gk_ref[...])))
    c0 += N_KV_HEADS_A
    for h in range(N_KV_HEADS_A):
        put_t(h, col(c0 + h))
    c0 += N_KV_HEADS_A
    for h in range(N_HEADS_B):
        put(k_ref, N_KV_HEADS_A + h, rope_b(col(c0 + h)))
    c0 += N_HEADS_B
    for h in range(N_HEADS_B):
        put_t(N_KV_HEADS_A + h, col(c0 + h))


Q_COLS = QA_COLS + QB_COLS
K_COLS = KA_COLS + KB_COLS
V_COLS = VA_COLS + VB_COLS


def _qkpost(p, g_q, g_k, rope_a, rope_b, n_lat):
    t = p.shape[0]
    n_tiles = t // ROW_TILE
    tiles_per_chunk = ATTN_TK // ROW_TILE
    assert tiles_per_chunk == 2 and (n_tiles + 1) % tiles_per_chunk == 0
    row = lambda w: pl.BlockSpec((ROW_TILE, w), lambda i: (jnp.minimum(i, n_tiles - 1), 0))
    vec = pl.BlockSpec((1, LANES), lambda i: (0, 0))
    vt_spec = pl.BlockSpec((1, V_COLS, ROW_TILE), lambda i: (i // tiles_per_chunk, 0, i % tiles_per_chunk))
    q, k, u, vt = pl.pallas_call(
        _qkpost_kernel,
        grid=(n_tiles + 1,),
        in_specs=[row(IN_COLS), vec, vec] + [row(LANES)] * 6,
        out_specs=[row(Q_COLS), row(K_COLS), row(C_COLS), vt_spec],
        out_shape=[jax.ShapeDtypeStruct((t, Q_COLS), BF16), jax.ShapeDtypeStruct((t, K_COLS), BF16),
                   jax.ShapeDtypeStruct((t, C_COLS), BF16),
                   jax.ShapeDtypeStruct((n_lat // ATTN_TK + 1, V_COLS, ATTN_TK), BF16)],
        compiler_params=_params("arbitrary"),
        name="qk_post",
    )(p, g_q.reshape(1, LANES), g_k.reshape(1, LANES), *rope_a, *rope_b)
    return q, k, u, vt


ATTN_TQ = 256
ATTN_TK = 512


REDUCE_ROWS = 32


def _stream_reduce(op, x):
    rows = x.shape[0]
    step = REDUCE_ROWS if rows % REDUCE_ROWS == 0 else rows
    acc = x[:step]
    for r in range(step, rows, step):
        acc = op(acc, x[r:r + step])
    red = jnp.max if op is jnp.maximum else jnp.sum
    return red(acc, axis=0, keepdims=True)


ATTN_TILES = 4
ONES_ROWS = 16


class _Tile:
    def __init__(self, refs, kv):
        self.qt, self.m, self.l, self.acc, s_a, c_a, s_b, c_b = refs
        self.stage = ((s_a, c_a), (s_b, c_b))
        self.kv = slice(kv * LANES, (kv + 1) * LANES)


def _attn_scratch():
    stat = pltpu.VMEM((1, ATTN_TQ), F32)
    score = pltpu.VMEM((ATTN_TK, ATTN_TQ), F32)
    per_tile = [pltpu.VMEM((LANES, ATTN_TQ), BF16), stat, stat, pltpu.VMEM((LANES, ATTN_TQ), F32),
                score, stat, score, stat]
    return per_tile * ATTN_TILES


def _make_tiles(scratch, kv_of_tile):
    n = len(scratch) // ATTN_TILES
    return [_Tile(scratch[n * i:n * (i + 1)], kv) for i, kv in enumerate(kv_of_tile)]


def _flash_t(tiles, k_ref, vt_ref, *, n_lat, n_ctx, is_lat):
    n_chunks = n_lat // ATTN_TK
    for t in tiles:
        t.m[...] = jnp.full(t.m.shape, NEG_BIG, F32)
        t.l[...] = jnp.zeros(t.l.shape, F32)
        t.acc[...] = jnp.zeros(t.acc.shape, F32)

    def scores(t, k, stage, rows):
        s = jnp.dot(k, t.qt[...], preferred_element_type=F32)
        s_ref, c_ref = t.stage[stage]
        s_ref[0:rows, :] = s
        c_ref[...] = _stream_reduce(jnp.maximum, s)

    def scores_lat(j, stage):
        off = pl.multiple_of(j * ATTN_TK, ATTN_TK)
        for t in tiles:
            scores(t, k_ref[pl.ds(off, ATTN_TK), t.kv], stage, ATTN_TK)

    def scores_ctx():
        for t in tiles:
            scores(t, k_ref[n_lat:n_lat + n_ctx, t.kv], 0, n_ctx)

    def consume(t, vt, stage, rows):
        s_ref, c_ref = t.stage[stage]
        m_old = t.m[...]
        m_new = jnp.maximum(m_old, c_ref[...])
        alpha = jnp.exp2(m_old - m_new)
        p = jnp.exp2(s_ref[0:rows, :] - m_new).astype(BF16)
        vt_ones = jnp.concatenate([vt, jnp.ones((ONES_ROWS, rows), BF16)], axis=0)
        pv = jnp.dot(vt_ones, p, preferred_element_type=F32)
        t.l[...] = alpha * t.l[...] + pv[LANES:LANES + 1]
        t.acc[...] = alpha * t.acc[...] + pv[:LANES]
        t.m[...] = m_new

    def consume_lat(j, stage):
        for t in tiles:
            consume(t, vt_ref[j, t.kv, :], stage, ATTN_TK)

    @pl.when(is_lat)
    def _():
        scores_lat(0, 0)

        def body(i, carry):
            j = 2 * i
            scores_lat(j + 1, 1)
            consume_lat(j, 0)
            scores_lat(j + 2, 0)
            consume_lat(j + 1, 1)
            return carry
        lax.fori_loop(0, n_chunks // 2 - 1, body, 0)
        scores_lat(n_chunks - 1, 1)
        consume_lat(n_chunks - 2, 0)
        scores_ctx()
        consume_lat(n_chunks - 1, 1)

    @pl.when(jnp.logical_not(is_lat))
    def _():
        scores_ctx()

    for t in tiles:
        consume(t, vt_ref[n_chunks, t.kv, 0:n_ctx], 0, n_ctx)


def _kv_specs(t, n_chunks, width, first_blk):
    return [pl.BlockSpec((t, width), lambda g, i: (0, first_blk + g)),
            pl.BlockSpec((n_chunks + 1, width, ATTN_TK), lambda g, i: (0, first_blk + g, 0))]


def _gqa_kernel(q_ref, k_ref, vt_ref, o_ref, *scratch, n_lat, n_ctx):
    tiles = _make_tiles(scratch, [0] * ATTN_TILES)
    for h, t in enumerate(tiles):
        t.qt[...] = q_ref[:, h * LANES:(h + 1) * LANES].astype(F32).T.astype(BF16)
    is_lat = pl.program_id(1) < n_lat // ATTN_TQ
    _flash_t(tiles, k_ref, vt_ref, n_lat=n_lat, n_ctx=n_ctx, is_lat=is_lat)
    for h, t in enumerate(tiles):
        o_ref[:, h * LANES:(h + 1) * LANES] = (t.acc[...] / t.l[...]).T.astype(o_ref.dtype)


def _gqa_attention(q, k, vt, n_lat, n_ctx, n_q_rows):
    group = N_HEADS_A // N_KV_HEADS_A
    assert group == ATTN_TILES
    return pl.pallas_call(
        functools.partial(_gqa_kernel, n_lat=n_lat, n_ctx=n_ctx),
        grid=(N_KV_HEADS_A, n_q_rows // ATTN_TQ),
        in_specs=[pl.BlockSpec((ATTN_TQ, group * LANES), lambda g, i: (i, g))]
        + _kv_specs(n_lat + n_ctx, n_lat // ATTN_TK, LANES, 0),
        out_specs=pl.BlockSpec((ATTN_TQ, group * LANES), lambda g, i: (i, g)),
        out_shape=jax.ShapeDtypeStruct((n_q_rows, QA_COLS), BF16),
        scratch_shapes=_attn_scratch(),
        compiler_params=_params("arbitrary", "arbitrary"),
        name="gqa_attention",
    )(q, k, vt)


DIFF_HEADS_PER_STEP = 2


def _diff_kernel(q_ref, k_ref, vt_ref, lam_ref, gs_ref, o_ref, *scratch,
                 n_lat, n_ctx, lambda_init):
    tiles = _make_tiles(scratch, [0, 0, 1, 1])
    q = q_ref[...].astype(F32)
    lane = lax.broadcasted_iota(jnp.int32, (ATTN_TQ, LANES), 1)
    for h in range(DIFF_HEADS_PER_STEP):
        qh = q[:, h * LANES:(h + 1) * LANES]
        zero = jnp.zeros_like(qh)
        tiles[2 * h].qt[...] = jnp.where(lane < HEAD_DIM_B, qh, zero).T.astype(BF16)
        tiles[2 * h + 1].qt[...] = jnp.where(lane >= HEAD_DIM_B, qh, zero).T.astype(BF16)
    is_lat = pl.program_id(1) < n_lat // ATTN_TQ
    _flash_t(tiles, k_ref, vt_ref, n_lat=n_lat, n_ctx=n_ctx, is_lat=is_lat)
    lv = lam_ref[...]
    lam = (jnp.exp(jnp.sum(lv[0:1] * lv[1:2], axis=-1, keepdims=True))
           - jnp.exp(jnp.sum(lv[2:3] * lv[3:4], axis=-1, keepdims=True)) + lambda_init)
    for h in range(DIFF_HEADS_PER_STEP):
        out1, out2 = (t.acc[...] / t.l[...] for t in tiles[2 * h:2 * h + 2])
        diff = (out1 - lam * out2).T
        o_ref[:, h * LANES:(h + 1) * LANES] = (
            _head_rmsnorm(diff, gs_ref[...]) * (1.0 - lambda_init)).astype(o_ref.dtype)


def _diff_attention(q, k, vt, lam_vecs, g_subln, lambda_init, n_lat, n_ctx, n_q_rows):
    width = DIFF_HEADS_PER_STEP * LANES
    assert 2 * DIFF_HEADS_PER_STEP == ATTN_TILES and QA_COLS % width == 0 and KA_COLS % width == 0
    return pl.pallas_call(
        functools.partial(_diff_kernel, n_lat=n_lat, n_ctx=n_ctx, lambda_init=lambda_init),
        grid=(N_HEADS_B // DIFF_HEADS_PER_STEP, n_q_rows // ATTN_TQ),
        in_specs=[pl.BlockSpec((ATTN_TQ, width), lambda h, i: (i, QA_COLS // width + h))]
        + _kv_specs(n_lat + n_ctx, n_lat // ATTN_TK, width, KA_COLS // width)
        + [pl.BlockSpec((4, HEAD_DIM_B), lambda h, i: (0, 0)),
           pl.BlockSpec((1, LANES), lambda h, i: (0, 0))],
        out_specs=pl.BlockSpec((ATTN_TQ, width), lambda h, i: (i, h)),
        out_shape=jax.ShapeDtypeStruct((n_q_rows, VB_COLS), BF16),
        scratch_shapes=_attn_scratch(),
        compiler_params=_params("arbitrary", "arbitrary"),
        name="diff_attention",
    )(q, k, vt, lam_vecs, g_subln.reshape(1, LANES))


def _dft_cos_sin(n):
    j = np.arange(n, dtype=np.float64)
    ang = 2.0 * np.pi * np.outer(j, j) / n
    return np.cos(ang), np.sin(ang)


def _channel_mix_weights(cs_ref, wf_ref, ab_ref):
    cs = cs_ref[...].astype(BF16)
    for g in range(N_FOURIER_GROUPS):
        ab_ref[g] = jnp.dot(cs, wf_ref[g].astype(BF16), preferred_element_type=F32).astype(BF16)


def _channel_mix(xr, xi, ab_ref, norm):
    outs = []
    for g in range(N_FOURIER_GROUPS):
        sl = slice(g * LANES, (g + 1) * LANES)
        lhs = jnp.concatenate([xr[:, sl], xi[:, sl]], axis=-1).astype(BF16)
        outs.append(jnp.dot(lhs, ab_ref[g], preferred_element_type=F32))
    return jnp.concatenate(outs, axis=-1) * norm


def _fourier_stage1_kernel(w_ref, u_ref, o_ref):
    o_ref[...] = jnp.dot(w_ref[...].astype(BF16), u_ref[...],
                         preferred_element_type=F32).astype(o_ref.dtype)


def _fourier_stage2_kernel(a_ref, tc_ref, ts_ref, w3_ref, cs_ref, wf_ref, o_ref, ab_ref, *, norm):
    @pl.when(pl.program_id(0) == 0)
    def _():
        _channel_mix_weights(cs_ref, wf_ref, ab_ref)

    ar = a_ref[0, 0].astype(F32)
    ai = a_ref[1, 0].astype(F32)
    tc, ts = tc_ref[0], ts_ref[0]
    br = ar * tc + ai * ts
    bi = ai * tc - ar * ts
    stack = jnp.concatenate([br, bi], axis=0).astype(BF16)
    x = jnp.dot(w3_ref[...].astype(BF16), stack, preferred_element_type=F32)
    half = x.shape[0] // 2
    o_ref[...] = _channel_mix(x[:half], x[half:], ab_ref, norm).astype(o_ref.dtype)


def _fourier_ctx_kernel(u_ref, w_ref, cs_ref, wf_ref, o_ref, ab_ref, *, norm):
    _channel_mix_weights(cs_ref, wf_ref, ab_ref)
    x = jnp.dot(w_ref[...].astype(BF16), u_ref[...], preferred_element_type=F32)
    half = x.shape[0] // 2
    o_ref[...] = _channel_mix(x[:half], x[half:], ab_ref, norm).astype(o_ref.dtype)


def _fourier_latent(u, n, w_fourier):
    assert u.shape[0] % LANES == 0
    n2 = LANES
    n1 = n // n2
    c1, s1 = _dft_cos_sin(n1)
    w1 = jnp.asarray(np.concatenate([c1, -s1], axis=0), F32)
    k1n2 = 2.0 * np.pi * np.outer(np.arange(n1), np.arange(n2)) / n
    tw_c = jnp.asarray(np.cos(k1n2)[:, :, None], F32)
    tw_s = jnp.asarray(np.sin(k1n2)[:, :, None], F32)
    c2, s2 = _dft_cos_sin(n2)
    w3 = jnp.asarray(np.block([[c2, s2], [-s2, c2]]), F32)
    cc, sc = _dft_cos_sin(FOURIER_GROUP_DIM)
    cs = jnp.asarray(np.concatenate([cc, sc], axis=0), F32)
    width = n2 * C_COLS
    cb = _largest_tile(width, (4096, 2048, 1024, 512))
    stage1 = pl.pallas_call(
        _fourier_stage1_kernel,
        grid=(width // cb,),
        in_specs=[pl.BlockSpec((2 * n1, n1), lambda j: (0, 0)),
                  pl.BlockSpec((n1, cb), lambda j: (0, j))],
        out_specs=pl.BlockSpec((2 * n1, cb), lambda j: (0, j)),
        out_shape=jax.ShapeDtypeStruct((2 * n1, width), BF16),
        compiler_params=_params("arbitrary"),
        name="fourier_stage1",
    )(w1, u.reshape(u.shape[0] // n2, width))
    a = stage1.reshape(2, n1, n2, C_COLS)
    norm = 1.0 / math.sqrt(n * FOURIER_GROUP_DIM)
    out = pl.pallas_call(
        functools.partial(_fourier_stage2_kernel, norm=norm),
        grid=(n1,),
        in_specs=[pl.BlockSpec((2, 1, n2, C_COLS), lambda k: (0, k, 0, 0)),
                  pl.BlockSpec((1, n2, 1), lambda k: (k, 0, 0)),
                  pl.BlockSpec((1, n2, 1), lambda k: (k, 0, 0)),
                  pl.BlockSpec((2 * n2, 2 * n2), lambda k: (0, 0)),
                  pl.BlockSpec((2 * FOURIER_GROUP_DIM, FOURIER_GROUP_DIM), lambda k: (0, 0)),
                  pl.BlockSpec((N_FOURIER_GROUPS, FOURIER_GROUP_DIM, FOURIER_GROUP_DIM),
                               lambda k: (0, 0, 0))],
        out_specs=pl.BlockSpec((n2, C_COLS), lambda k: (0, k)),
        out_shape=jax.ShapeDtypeStruct((n2, n1 * C_COLS), BF16),
        scratch_shapes=[pltpu.VMEM((N_FOURIER_GROUPS, 2 * FOURIER_GROUP_DIM, FOURIER_GROUP_DIM), BF16)],
        compiler_params=_params("arbitrary"),
        name="fourier_stage2",
    )(a, tw_c, tw_s, w3, cs, w_fourier)
    return out.reshape(n, C_COLS)


def _fourier_ctx(u_ctx, w_fourier):
    m = u_ctx.shape[0]
    c, s = _dft_cos_sin(m)
    w = jnp.asarray(np.concatenate([c, -s], axis=0), F32)
    cc, sc = _dft_cos_sin(FOURIER_GROUP_DIM)
    cs = jnp.asarray(np.concatenate([cc, sc], axis=0), F32)
    norm = 1.0 / math.sqrt(m * FOURIER_GROUP_DIM)
    return pl.pallas_call(
        functools.partial(_fourier_ctx_kernel, norm=norm),
        out_shape=jax.ShapeDtypeStruct((m, C_COLS), BF16),
        scratch_shapes=[pltpu.VMEM((N_FOURIER_GROUPS, 2 * FOURIER_GROUP_DIM, FOURIER_GROUP_DIM), BF16)],
        compiler_params=pltpu.CompilerParams(vmem_limit_bytes=VMEM_LIMIT),
        name="fourier_ctx",
    )(u_ctx, w, cs, w_fourier)


def _gated_residual(x, f, g, mod, gate_row):
    ms = jnp.mean(f * f, axis=-1, keepdims=True)
    return x + mod[gate_row:gate_row + 1] * (f * lax.rsqrt(ms + NORM_EPS) * g)


def _residual_kernel(x_ref, f_ref, g_ref, mod_ref, o_ref, *, gate_row):
    o_ref[...] = _gated_residual(x_ref[...], f_ref[...], g_ref[...], mod_ref[0], gate_row)


def _post_residual(x, f, g, mod, gate_row, n_rows, n_lat):
    d = x.shape[1]
    row = pl.BlockSpec((ROW_TILE, d), lambda i: (i, 0))
    return pl.pallas_call(
        functools.partial(_residual_kernel, gate_row=gate_row),
        grid=(n_rows // ROW_TILE,),
        in_specs=[row, row, pl.BlockSpec((1, d), lambda i: (0, 0)), _mod_spec(d, n_lat // ROW_TILE)],
        out_specs=row,
        out_shape=jax.ShapeDtypeStruct((n_rows, d), F32),
        compiler_params=_params("arbitrary"),
        name="post_residual",
    )(x, f, g.reshape(1, d), mod)


MOE_TM = 256
DISPATCH_TM = 2 * MOE_TM
R_IDX, R_W, R_RANK = 0, 2, 4


def _router_kernel(x_ref, g_ref, mod_ref, wr_ref, h_ref, route_ref, cnt_ref, carry_ref, *, shift_row):
    @pl.when(pl.program_id(0) == 0)
    def _():
        carry_ref[...] = jnp.zeros(carry_ref.shape, F32)

    h = _norm_modulate(x_ref[...], g_ref[...], mod_ref[0], shift_row)
    h_ref[...] = h
    logits = jnp.dot(h, wr_ref[...], preferred_element_type=F32, precision=lax.Precision.HIGHEST)
    rows = logits.shape[0]
    lane = lax.broadcasted_iota(jnp.int32, logits.shape, 1).astype(F32)
    neg = jnp.full(logits.shape, -jnp.inf, F32)
    l1 = jnp.where(lane < N_EXPERTS, logits, neg)
    m1 = jnp.max(l1, axis=-1, keepdims=True)
    i1 = jnp.min(jnp.where(l1 == m1, lane, float(LANES)), axis=-1, keepdims=True)
    l2 = jnp.where(lane == i1, neg, l1)
    m2 = jnp.max(l2, axis=-1, keepdims=True)
    i2 = jnp.min(jnp.where(l2 == m2, lane, float(LANES)), axis=-1, keepdims=True)
    e2 = jnp.exp(m2 - m1)
    w1 = 1.0 / (1.0 + e2)
    w2 = e2 / (1.0 + e2)
    hit1 = lane == i1
    hit2 = lane == i2
    onehot = jnp.logical_or(hit1, hit2).astype(BF16)
    r_i = lax.broadcasted_iota(jnp.int32, (rows, rows), 0)
    c_i = lax.broadcasted_iota(jnp.int32, (rows, rows), 1)
    before = (c_i < r_i).astype(BF16)
    base = carry_ref[...] + jnp.dot(before, onehot, preferred_element_type=F32)
    rank1 = jnp.sum(jnp.where(hit1, base, 0.0), axis=-1, keepdims=True)
    rank2 = jnp.sum(jnp.where(hit2, base, 0.0), axis=-1, keepdims=True)
    carry_ref[...] = carry_ref[...] + jnp.sum(onehot.astype(F32), axis=0, keepdims=True)
    rec = jnp.zeros(logits.shape, F32)
    for off, val in ((R_IDX, i1), (R_IDX + 1, i2), (R_W, w1), (R_W + 1, w2),
                     (R_RANK, rank1), (R_RANK + 1, rank2)):
        rec = jnp.where(lane == off, val, rec)
    route_ref[...] = rec
    cnt_ref[...] = jnp.broadcast_to(carry_ref[...], cnt_ref.shape)


def _router(x, g, mod, w_router, n_lat):
    d = x.shape[1]
    wr = jnp.zeros((d, LANES), F32).at[:, :N_EXPERTS].set(w_router)
    return pl.pallas_call(
        functools.partial(_router_kernel, shift_row=3),
        grid=(n_lat // ROW_TILE,),
        in_specs=[pl.BlockSpec((ROW_TILE, d), lambda i: (i, 0)),
                  pl.BlockSpec((1, d), lambda i: (0, 0)),
                  pl.BlockSpec((1, N_MOD, d), lambda i: (0, 0, 0)),
                  pl.BlockSpec((d, LANES), lambda i: (0, 0))],
        out_specs=[pl.BlockSpec((ROW_TILE, d), lambda i: (i, 0)),
                   pl.BlockSpec((ROW_TILE, LANES), lambda i: (i, 0)),
                   pl.BlockSpec((8, LANES), lambda i: (0, 0))],
        out_shape=[jax.ShapeDtypeStruct((n_lat, d), F32),
                   jax.ShapeDtypeStruct((n_lat, LANES), F32),
                   jax.ShapeDtypeStruct((8, LANES), F32)],
        scratch_shapes=[pltpu.VMEM((1, LANES), F32)],
        compiler_params=_params("arbitrary"),
        name="router",
    )(x, g.reshape(1, d), mod, wr)


def _row_copy(src_hbm, src_row, dst_ref, dst_row, sem):
    return pltpu.make_async_copy(src_hbm.at[pl.ds(src_row, 1)], dst_ref.at[pl.ds(dst_row, 1)], sem)


def _wait_rows(src_hbm, dst_ref, sem):
    pltpu.make_async_copy(src_hbm.at[pl.ds(0, dst_ref.shape[0])], dst_ref, sem).wait()


ISSUE_UNROLL = 8


def _dispatch_kernel(src_ref, nv_ref, h_hbm, o_ref, buf_ref, sems):
    i = pl.program_id(0)
    n_valid = nv_ref[0]
    slot = lax.rem(i, 2)

    def issue(tile, slot):
        base = tile * DISPATCH_TM

        def body(r, carry):
            _row_copy(h_hbm, src_ref[base + r], buf_ref.at[slot], r, sems.at[slot]).start()
            return carry
        lax.fori_loop(0, DISPATCH_TM, body, 0, unroll=ISSUE_UNROLL)

    @pl.when(jnp.logical_and(i == 0, n_valid > 0))
    def _():
        issue(0, 0)

    @pl.when(i + 1 < n_valid)
    def _():
        issue(i + 1, 1 - slot)

    @pl.when(i < n_valid)
    def _():
        _wait_rows(h_hbm, buf_ref.at[slot], sems.at[slot])
        o_ref[...] = buf_ref[slot].astype(o_ref.dtype)

    @pl.when(i >= n_valid)
    def _():
        o_ref[...] = jnp.zeros(o_ref.shape, o_ref.dtype)


def _dispatch(src_token, n_valid, h):
    d = h.shape[1]
    n_slots = src_token.shape[0]
    grid_spec = pltpu.PrefetchScalarGridSpec(
        num_scalar_prefetch=2,
        grid=(n_slots // DISPATCH_TM,),
        in_specs=[pl.BlockSpec(memory_space=pl.ANY)],
        out_specs=pl.BlockSpec((DISPATCH_TM, d), lambda i, src, nv: (i, 0)),
        scratch_shapes=[pltpu.VMEM((2, DISPATCH_TM, d), F32), pltpu.SemaphoreType.DMA((2,))],
    )
    return pl.pallas_call(
        _dispatch_kernel,
        grid_spec=grid_spec,
        out_shape=jax.ShapeDtypeStruct((n_slots, d), BF16),
        compiler_params=_params("arbitrary"),
        name="moe_dispatch",
    )(src_token, n_valid, h)


def _combine_kernel(dest_ref, y_hbm, route_ref, x_ref, g_ref, mod_ref, o_ref, buf_ref, sems, *, gate_row):
    i = pl.program_id(0)
    slot = lax.rem(i, 2)

    def issue(tile, slot):
        base = tile * ROW_TILE

        def body(r, carry):
            for k in range(TOP_K):
                _row_copy(y_hbm, dest_ref[TOP_K * (base + r) + k], buf_ref.at[slot], k * ROW_TILE + r,
                          sems.at[slot]).start()
            return carry
        lax.fori_loop(0, ROW_TILE, body, 0, unroll=ISSUE_UNROLL)

    @pl.when(i == 0)
    def _():
        issue(0, 0)

    @pl.when(i + 1 < pl.num_programs(0))
    def _():
        issue(i + 1, 1 - slot)

    _wait_rows(y_hbm, buf_ref.at[slot], sems.at[slot])
    route = route_ref[...]
    f = (route[:, R_W:R_W + 1] * buf_ref[slot, 0:ROW_TILE, :]
         + route[:, R_W + 1:R_W + 2] * buf_ref[slot, ROW_TILE:TOP_K * ROW_TILE, :])
    o_ref[...] = _gated_residual(x_ref[...], f, g_ref[...], mod_ref[0], gate_row)


def _combine_residual(dest_flat, y_sorted, route, x, g, mod, gate_row, n_lat):
    d = x.shape[1]
    row = lambda w: pl.BlockSpec((ROW_TILE, w), lambda i, dest: (i, 0))
    grid_spec = pltpu.PrefetchScalarGridSpec(
        num_scalar_prefetch=1,
        grid=(n_lat // ROW_TILE,),
        in_specs=[pl.BlockSpec(memory_space=pl.ANY), row(LANES), row(d),
                  pl.BlockSpec((1, d), lambda i, dest: (0, 0)),
                  pl.BlockSpec((1, N_MOD, d), lambda i, dest: (0, 0, 0))],
        out_specs=row(d),
        scratch_shapes=[pltpu.VMEM((2, TOP_K * ROW_TILE, d), F32), pltpu.SemaphoreType.DMA((2,))],
    )
    return pl.pallas_call(
        functools.partial(_combine_kernel, gate_row=gate_row),
        grid_spec=grid_spec,
        out_shape=jax.ShapeDtypeStruct((n_lat, d), F32),
        compiler_params=_params("arbitrary"),
        name="moe_combine",
    )(dest_flat, y_sorted, route, x, g.reshape(1, d), mod)


def _tile_meta(starts, counts, ends, n_slots, tm):
    tile_start = jnp.arange(n_slots // tm, dtype=jnp.int32) * tm
    group = jnp.minimum(jnp.sum((tile_start[:, None] >= ends[None, :]).astype(jnp.int32), axis=1),
                        N_EXPERTS - 1)
    rows = jnp.clip(starts[group] + counts[group] - tile_start, 0, tm)
    return group, rows


def _moe_layer(x, g_pre, g_post, mod, w_router, w_gate, w_up, w_down, n_lat):
    d_ff = w_gate.shape[-1]
    h, route, cnt = _router(x, g_pre, mod, w_router, n_lat)
    experts = route[:, R_IDX:R_IDX + TOP_K].astype(jnp.int32)
    ranks = route[:, R_RANK:R_RANK + TOP_K].astype(jnp.int32)
    counts = cnt[0, :N_EXPERTS].astype(jnp.int32)
    padded = ((counts + MOE_TM - 1) // MOE_TM) * MOE_TM
    ends = jnp.cumsum(padded)
    starts = ends - padded
    dest = (starts[experts] + ranks).reshape(-1)
    n_slots = TOP_K * n_lat + N_EXPERTS * MOE_TM
    assert n_slots % DISPATCH_TM == 0
    n_valid = ((ends[-1:] + DISPATCH_TM - 1) // DISPATCH_TM).astype(jnp.int32)

    token = jnp.repeat(jnp.arange(n_lat, dtype=jnp.int32), TOP_K)
    src_token = jnp.zeros((n_slots,), jnp.int32).at[dest].set(token)
    hs = _dispatch(src_token, n_valid, h)
    group, rows = _tile_meta(starts, counts, ends, n_slots, MOE_TM)
    act = _gmm(hs, [w_gate, w_up], group, rows, tm=MOE_TM,
               tn=_largest_tile(d_ff, (1024, 512, 256, 128)), out_dtype=BF16, swiglu=True, name="moe_gate_up")
    y = _gmm(act, [w_down], group, rows, tm=MOE_TM,
             tn=_largest_tile(x.shape[1], (512, 256, 128)), out_dtype=F32, name="moe_down")
    return _combine_residual(dest, y, route, x, g_post, mod, 5, n_lat)


def kernel(x, c, ctx, c_ctx, w_mod, b_mod, g_pre_mix, g_post_mix, g_pre_ffn, g_post_ffn, w_in, g_q_a, g_k_a, lambda_q1, lambda_k1, lambda_q2, lambda_k2, g_subln_b, w_fourier, w_out, w_gate_dense, w_up_dense, w_down_dense, w_router, w_gate_moe, w_up_moe, w_down_moe):
    batch, n_lat, d = x.shape
    n_ctx = ctx.shape[1]
    depth = w_mod.shape[0]
    assert batch == 1 and n_ctx == ROW_TILE and n_lat % max(2 * ATTN_TK, MOE_TM, GRID_W) == 0
    assert depth == 2, "layer 0 dense, layer 1 (last) routed experts"
    t = n_lat + n_ctx

    cond_rows = jnp.concatenate([c, c_ctx[None, :], jnp.zeros((6, d), F32)], axis=0)
    mod_all = _modulation(cond_rows, w_mod, b_mod)
    rope_a = _rope_tables(n_lat, n_ctx, HEAD_DIM_A, 1)
    rope_b = _rope_tables(n_lat, n_ctx, HEAD_DIM_B, 2)

    xs = jnp.concatenate([x[0], ctx[0]], axis=0)
    for layer in range(depth):
        last = layer == depth - 1
        n_out = n_lat if last else t
        mod = mod_all[layer, :2].reshape(2, N_MOD, d)
        lambda_init = 0.8 - 0.6 * math.exp(-0.3 * layer)

        h = _prenorm(xs, g_pre_mix[layer], mod, 0, t, n_lat)
        p = _dense_matmul(h, w_in[layer], tn=512, out_dtype=F32, name="in_proj")
        q, k, u, vt = _qkpost(p, g_q_a[layer], g_k_a[layer], rope_a, rope_b, n_lat)
        y_a = _gqa_attention(q, k, vt, n_lat, n_ctx, n_out)
        lam_vecs = jnp.stack([lambda_q1[layer], lambda_k1[layer], lambda_q2[layer], lambda_k2[layer]])
        y_b = _diff_attention(q, k, vt, lam_vecs, g_subln_b[layer], lambda_init,
                              n_lat, n_ctx, n_out)
        y_c = _fourier_latent(u, n_lat, w_fourier[layer])
        if not last:
            y_c = jnp.concatenate([y_c, _fourier_ctx(u[n_lat:], w_fourier[layer])], axis=0)
        f = _dense_matmul([y_a, y_b, y_c], w_out[layer], tn=512, out_dtype=F32, name="out_proj")
        xs = _post_residual(xs, f, g_post_mix[layer], mod, 2, n_out, n_lat)

        j = layer // 2
        if layer % 2 == 0:
            h = _prenorm(xs, g_pre_ffn[layer], mod, 3, n_out, n_lat)
            act = _dense_matmul(h, w_gate_dense[j], w2=w_up_dense[j], swiglu=True, tn=512,
                                out_dtype=BF16, name="ffn_gate_up")
            f = _dense_matmul(act, w_down_dense[j], tn=512, tiles=(ROW_TILE,), out_dtype=F32, name="ffn_down")
            xs = _post_residual(xs, f, g_post_ffn[layer], mod, 5, n_out, n_lat)
        else:
            xs = _moe_layer(xs, g_pre_ffn[layer], g_post_ffn[layer], mod, w_router[j],
                            w_gate_moe[j], w_up_moe[j], w_down_moe[j], n_lat)
    return xs[:n_lat][None]
```

```python
import functools
import math

import numpy as np
import jax
import jax.numpy as jnp
from jax import lax
from jax.experimental import pallas as pl
from jax.experimental.pallas import tpu as pltpu

F32 = jnp.float32
BF16 = jnp.bfloat16

GRID_W = 64
ROPE_THETA = 10000.0
NORM_EPS = 1e-6
N_HEADS_A, N_KV_HEADS_A, HEAD_DIM_A = 8, 2, 128
N_HEADS_B, HEAD_DIM_B = 4, 64
VALUE_DIM_B = 2 * HEAD_DIM_B
N_FOURIER_GROUPS, FOURIER_GROUP_DIM = 4, 128
QA_COLS = N_HEADS_A * HEAD_DIM_A
QB_COLS = N_HEADS_B * 2 * HEAD_DIM_B
C_COLS = N_FOURIER_GROUPS * FOURIER_GROUP_DIM
KA_COLS = N_KV_HEADS_A * HEAD_DIM_A
VA_COLS = KA_COLS
KB_COLS = QB_COLS
VB_COLS = N_HEADS_B * VALUE_DIM_B
KV_START = QA_COLS + QB_COLS + C_COLS
IN_COLS = KV_START + KA_COLS + VA_COLS + KB_COLS + VB_COLS
N_EXPERTS, TOP_K = 8, 2
N_MOD = 6

LANES = 128
V7X_VMEM_BYTES = 64 * 1024 * 1024
VMEM_LIMIT = 52 * 1024 * 1024

ROW_TILE = 256
NEG_BIG = -1e30
LOG2_E = math.log2(math.e)


def _params(*sem):
    return pltpu.CompilerParams(dimension_semantics=sem, vmem_limit_bytes=VMEM_LIMIT)


def _silu(x):
    return x / (1.0 + jnp.exp(-x))


def _largest_tile(n, candidates):
    for c in candidates:
        if n % c == 0:
            return c
    raise ValueError(f"no tile in {candidates} divides {n}")


def _mod_kernel(c_ref, w_ref, b_ref, o_ref):
    cond = _silu(c_ref[...])
    o_ref[0] = jnp.dot(cond.astype(BF16), w_ref[0].astype(BF16),
                       preferred_element_type=F32) + b_ref[0]


def _modulation(cond_rows, w_mod, b_mod):
    depth, d, n6 = w_mod.shape
    tn = _largest_tile(n6, (1024, 512, 256, 128))
    return pl.pallas_call(
        _mod_kernel,
        grid=(depth, n6 // tn),
        in_specs=[pl.BlockSpec((8, d), lambda l, j: (0, 0)),
                  pl.BlockSpec((1, d, tn), lambda l, j: (l, 0, j)),
                  pl.BlockSpec((1, 1, tn), lambda l, j: (l, 0, j))],
        out_specs=pl.BlockSpec((1, 8, tn), lambda l, j: (l, 0, j)),
        out_shape=jax.ShapeDtypeStruct((depth, 8, n6), F32),
        compiler_params=_params("arbitrary", "arbitrary"),
        name="modulation",
    )(cond_rows, w_mod, b_mod.reshape(depth, 1, n6))


def _norm_modulate(x, g, mod, shift_row):
    ms = jnp.mean(x * x, axis=-1, keepdims=True)
    y = x * lax.rsqrt(ms + NORM_EPS) * g
    return y * (1.0 + mod[shift_row + 1:shift_row + 2]) + mod[shift_row:shift_row + 1]


def _prenorm_kernel(x_ref, g_ref, mod_ref, o_ref, *, shift_row):
    h = _norm_modulate(x_ref[...], g_ref[...], mod_ref[0], shift_row)
    o_ref[...] = h.astype(o_ref.dtype)


def _mod_spec(d, n_lat_tiles):
    return pl.BlockSpec((1, N_MOD, d), lambda i: ((i >= n_lat_tiles).astype(jnp.int32), 0, 0))


def _prenorm(x, g, mod, shift_row, n_rows, n_lat):
    d = x.shape[1]
    return pl.pallas_call(
        functools.partial(_prenorm_kernel, shift_row=shift_row),
        grid=(n_rows // ROW_TILE,),
        in_specs=[pl.BlockSpec((ROW_TILE, d), lambda i: (i, 0)),
                  pl.BlockSpec((1, d), lambda i: (0, 0)),
                  _mod_spec(d, n_lat // ROW_TILE)],
        out_specs=pl.BlockSpec((ROW_TILE, d), lambda i: (i, 0)),
        out_shape=jax.ShapeDtypeStruct((n_rows, d), BF16),
        compiler_params=_params("arbitrary"),
        name="prenorm",
    )(x, g.reshape(1, d), mod)


def _gmm_kernel(tg_ref, rows_ref, lhs_ref, *refs, n_w, swiglu):
    w_refs, (o_ref, wb_ref) = refs[:n_w], refs[n_w:]
    m = pl.program_id(1)
    first = jnp.logical_or(m == 0, tg_ref[m] != tg_ref[jnp.maximum(m - 1, 0)])

    @pl.when(first)
    def _():
        for i in range(n_w):
            wb_ref[i] = w_refs[i][0].astype(BF16)

    live = rows_ref[m] > 0

    @pl.when(live)
    def _():
        x = lhs_ref[...]
        if swiglu:
            gate = jnp.dot(x, wb_ref[0], preferred_element_type=F32)
            up = jnp.dot(x, wb_ref[1], preferred_element_type=F32)
            out = _silu(gate) * up
        else:
            out = jnp.dot(x, wb_ref[0], preferred_element_type=F32)
        o_ref[...] = out.astype(o_ref.dtype)

    @pl.when(jnp.logical_not(live))
    def _():
        o_ref[...] = jnp.zeros(o_ref.shape, o_ref.dtype)


def _gmm(lhs, weights, tile_group, tile_rows, *, tm, tn, out_dtype, swiglu=False, n_rows=None, name="gmm"):
    n_rows = lhs.shape[0] if n_rows is None else n_rows
    _, k, nw = weights[0].shape
    n_w = len(weights)
    grid_spec = pltpu.PrefetchScalarGridSpec(
        num_scalar_prefetch=2,
        grid=(nw // tn, n_rows // tm),
        in_specs=[pl.BlockSpec((tm, k), lambda n, m, tg, tr: (m, 0))]
        + [pl.BlockSpec((1, k, tn), lambda n, m, tg, tr: (tg[m], 0, n)) for _ in range(n_w)],
        out_specs=pl.BlockSpec((tm, tn), lambda n, m, tg, tr: (m, n)),
        scratch_shapes=[pltpu.VMEM((n_w, k, tn), BF16)],
    )
    return pl.pallas_call(
        functools.partial(_gmm_kernel, n_w=n_w, swiglu=swiglu),
        grid_spec=grid_spec,
        out_shape=jax.ShapeDtypeStruct((n_rows, nw), out_dtype),
        compiler_params=_params("arbitrary", "arbitrary"),
        name=name,
    )(tile_group, tile_rows, lhs, *weights)


def _dense_matmul(lhs, w, *, tn, out_dtype, tiles=(1024, 768, 512, 256), swiglu=False, w2=None,
                  n_rows=None, name="matmul"):
    n_rows = lhs.shape[0] if n_rows is None else n_rows
    tm = _largest_tile(n_rows, tiles)
    n_tiles = n_rows // tm
    weights = [w[None]] if w2 is None else [w[None], w2[None]]
    return _gmm(lhs, weights, jnp.zeros((n_tiles,), jnp.int32), jnp.full((n_tiles,), tm, jnp.int32),
                tm=tm, tn=tn, out_dtype=out_dtype, swiglu=swiglu, n_rows=n_rows, name=name)


def _rope_tables(n_lat, n_ctx, head_dim, reps):
    n_rows_grid = n_lat // GRID_W
    rows = jnp.repeat(jnp.arange(n_rows_grid), GRID_W).astype(F32)
    cols = jnp.tile(jnp.arange(GRID_W), n_rows_grid).astype(F32)
    axis_dim = head_dim // 2
    freqs = ROPE_THETA ** (-jnp.arange(0, axis_dim, 2, dtype=F32) / axis_dim)
    ra, ca = rows[:, None] * freqs, cols[:, None] * freqs
    rc, rs, cc, cs = jnp.cos(ra), jnp.sin(ra), jnp.cos(ca), jnp.sin(ca)
    z = jnp.zeros_like(rs)
    cos = jnp.concatenate([rc, rc, cc, cc], axis=-1)
    sin_up = jnp.concatenate([-rs, z, -cs, z], axis=-1)
    sin_dn = jnp.concatenate([z, rs, z, cs], axis=-1)
    pad = lambda t, v: jnp.concatenate(
        [jnp.tile(t, (1, reps)), jnp.full((n_ctx, LANES), v, F32)], axis=0)
    return pad(cos, 1.0), pad(sin_up, 0.0), pad(sin_dn, 0.0)


def _rope(x, cos, sin_up, sin_dn, quarter):
    return (x * cos + pltpu.roll(x, LANES - quarter, 1) * sin_up
            + pltpu.roll(x, quarter, 1) * sin_dn)


def _head_rmsnorm(x, g):
    return x * lax.rsqrt(jnp.mean(x * x, axis=-1, keepdims=True) + NORM_EPS) * g


def _qkpost_kernel(p_ref, gq_ref, gk_ref, ca_ref, ua_ref, da_ref, cb_ref, ub_ref, db_ref,
                   q_ref, k_ref, u_ref, vt_ref):
    def col(c):
        return p_ref[:, c * LANES:(c + 1) * LANES]

    def put(ref, c, x):
        ref[:, c * LANES:(c + 1) * LANES] = x.astype(BF16)

    is_tail = pl.program_id(0) == pl.num_programs(0) - 1

    def put_t(c, x):
        xt = x.T
        vt_ref[0, c * LANES:(c + 1) * LANES, :] = jnp.where(is_tail, jnp.zeros_like(xt), xt).astype(BF16)

    rope_a = lambda x: _rope(x, ca_ref[...], ua_ref[...], da_ref[...], HEAD_DIM_A // 4)
    rope_b = lambda x: _rope(x, cb_ref[...], ub_ref[...], db_ref[...], HEAD_DIM_B // 4)
    scale_a, scale_b = LOG2_E * HEAD_DIM_A ** -0.5, LOG2_E * HEAD_DIM_B ** -0.5
    c0 = 0
    for h in range(N_HEADS_A):
        put(q_ref, h, rope_a(_head_rmsnorm(col(c0 + h), gq_ref[...])) * scale_a)
    c0 += N_HEADS_A
    for h in range(N_HEADS_B):
        put(q_ref, N_HEADS_A + h, rope_b(col(c0 + h)) * scale_b)
    c0 += N_HEADS_B
    for g in range(N_FOURIER_GROUPS):
        put(u_ref, g, col(c0 + g))
    c0 += N_FOURIER_GROUPS
    for h in range(N_KV_HEADS_A):
        put(k_ref, h, rope_a(_head_rmsnorm(col(c0 + h), gk_ref[...])))
    c0 += N_KV_HEADS_A
    for h in range(N_KV_HEADS_A):
        put_t(h, col(c0 + h))
    c0 += N_KV_HEADS_A
    for h in range(N_HEADS_B):
        put(k_ref, N_KV_HEADS_A + h, rope_b(col(c0 + h)))
    c0 += N_HEADS_B
    for h in range(N_HEADS_B):
        put_t(N_KV_HEADS_A + h, col(c0 + h))


Q_COLS = QA_COLS + QB_COLS
K_COLS = KA_COLS + KB_COLS
V_COLS = VA_COLS + VB_COLS


def _mixer_front_kernel(x_ref, g_ref, mod_ref, w_ref, *rest):
    p_ref = rest[-1]
    h = _norm_modulate(x_ref[...], g_ref[...], mod_ref[0], 0)
    p_ref[...] = jnp.dot(h.astype(BF16), w_ref[...], preferred_element_type=F32)
    _qkpost_kernel(p_ref, *rest[:-1])


def _mixer_front(x, g, mod, w_in_bf16, g_q, g_k, rope_a, rope_b, n_lat):
    t, d = x.shape
    n_tiles = t // ROW_TILE
    n_lat_tiles = n_lat // ROW_TILE
    tiles_per_chunk = ATTN_TK // ROW_TILE
    assert tiles_per_chunk == 2 and (n_tiles + 1) % tiles_per_chunk == 0
    row = lambda w: pl.BlockSpec((ROW_TILE, w), lambda i: (jnp.minimum(i, n_tiles - 1), 0))
    vec = lambda w: pl.BlockSpec((1, w), lambda i: (0, 0))
    vt_spec = pl.BlockSpec((1, V_COLS, ROW_TILE), lambda i: (i // tiles_per_chunk, 0, i % tiles_per_chunk))
    return pl.pallas_call(
        _mixer_front_kernel,
        grid=(n_tiles + 1,),
        in_specs=[row(d), vec(d), _mod_spec(d, n_lat_tiles), pl.BlockSpec((d, IN_COLS), lambda i: (0, 0)),
                  vec(LANES), vec(LANES)] + [row(LANES)] * 6,
        out_specs=[row(Q_COLS), row(K_COLS), row(C_COLS), vt_spec],
        out_shape=[jax.ShapeDtypeStruct((t, Q_COLS), BF16), jax.ShapeDtypeStruct((t, K_COLS), BF16),
                   jax.ShapeDtypeStruct((t, C_COLS), BF16),
                   jax.ShapeDtypeStruct((n_lat // ATTN_TK + 1, V_COLS, ATTN_TK), BF16)],
        scratch_shapes=[pltpu.VMEM((ROW_TILE, IN_COLS), F32)],
        compiler_params=_params("arbitrary"),
        name="mixer_front",
    )(x, g.reshape(1, d), mod, w_in_bf16, g_q.reshape(1, LANES), g_k.reshape(1, LANES), *rope_a, *rope_b)


ATTN_TQ = 256
ATTN_TK = 512


REDUCE_ROWS = 32


def _stream_reduce(op, x):
    rows = x.shape[0]
    step = REDUCE_ROWS if rows % REDUCE_ROWS == 0 else rows
    acc = x[:step]
    for r in range(step, rows, step):
        acc = op(acc, x[r:r + step])
    red = jnp.max if op is jnp.maximum else jnp.sum
    return red(acc, axis=0, keepdims=True)


ONES_ROWS = 16


class _Tile:
    def __init__(self, refs, kv):
        self.qt, self.m, self.l, self.acc, s_a, c_a, s_b, c_b = refs
        self.stage = ((s_a, c_a), (s_b, c_b))
        self.kv = slice(kv * LANES, (kv + 1) * LANES)


def _attn_scratch(n_tiles):
    stat = pltpu.VMEM((1, ATTN_TQ), F32)
    score = pltpu.VMEM((ATTN_TK, ATTN_TQ), F32)
    per_tile = [pltpu.VMEM((LANES, ATTN_TQ), BF16), stat, stat, pltpu.VMEM((LANES, ATTN_TQ), F32),
                score, stat, score, stat]
    return per_tile * n_tiles


def _make_tiles(scratch, kv_of_tile):
    n = len(scratch) // len(kv_of_tile)
    return [_Tile(scratch[n * i:n * (i + 1)], kv) for i, kv in enumerate(kv_of_tile)]


def _flash_t(tiles, k_ref, vt_ref, *, n_lat, n_ctx, latent):
    n_chunks = n_lat // ATTN_TK
    for t in tiles:
        t.m[...] = jnp.full(t.m.shape, NEG_BIG, F32)
        t.l[...] = jnp.zeros(t.l.shape, F32)
        t.acc[...] = jnp.zeros(t.acc.shape, F32)

    def scores(t, k, stage, rows):
        s = jnp.dot(k, t.qt[...], preferred_element_type=F32)
        s_ref, c_ref = t.stage[stage]
        s_ref[0:rows, :] = s
        c_ref[...] = _stream_reduce(jnp.maximum, s)

    def scores_lat(j, stage):
        off = pl.multiple_of(j * ATTN_TK, ATTN_TK)
        for t in tiles:
            scores(t, k_ref[pl.ds(off, ATTN_TK), t.kv], stage, ATTN_TK)

    def scores_ctx():
        for t in tiles:
            scores(t, k_ref[n_lat:n_lat + n_ctx, t.kv], 0, n_ctx)

    def consume(t, vt, stage, rows):
        s_ref, c_ref = t.stage[stage]
        m_old = t.m[...]
        m_new = jnp.maximum(m_old, c_ref[...])
        alpha = jnp.exp2(m_old - m_new)
        p = jnp.exp2(s_ref[0:rows, :] - m_new).astype(BF16)
        vt_ones = jnp.concatenate([vt, jnp.ones((ONES_ROWS, rows), BF16)], axis=0)
        pv = jnp.dot(vt_ones, p, preferred_element_type=F32)
        t.l[...] = alpha * t.l[...] + pv[LANES:LANES + 1]
        t.acc[...] = alpha * t.acc[...] + pv[:LANES]
        t.m[...] = m_new

    def consume_lat(j, stage):
        for t in tiles:
            consume(t, vt_ref[j, t.kv, :], stage, ATTN_TK)

    if latent:
        scores_lat(0, 0)

        def body(i, carry):
            j = 2 * i
            scores_lat(j + 1, 1)
            consume_lat(j, 0)
            scores_lat(j + 2, 0)
            consume_lat(j + 1, 1)
            return carry
        lax.fori_loop(0, n_chunks // 2 - 1, body, 0)
        scores_lat(n_chunks - 1, 1)
        consume_lat(n_chunks - 2, 0)
        scores_ctx()
        consume_lat(n_chunks - 1, 1)
    else:
        scores_ctx()
    for t in tiles:
        consume(t, vt_ref[n_chunks, t.kv, 0:n_ctx], 0, n_ctx)


def _kv_specs(t, n_chunks, width, first_blk):
    return [pl.BlockSpec((t, width), lambda g, i: (0, first_blk + g)),
            pl.BlockSpec((n_chunks + 1, width, ATTN_TK), lambda g, i: (0, first_blk + g, 0))]


LATENT_ROW_BLOCKS = 2


def _row_block(rb):
    return slice(rb * ATTN_TQ, (rb + 1) * ATTN_TQ)


def _gqa_kernel(q_ref, k_ref, vt_ref, o_ref, *scratch, n_lat, n_ctx, latent, row_blocks):
    group = N_HEADS_A // N_KV_HEADS_A
    tiles = _make_tiles(scratch, [0] * (row_blocks * group))
    for rb in range(row_blocks):
        for h in range(group):
            tiles[rb * group + h].qt[...] = (
                q_ref[_row_block(rb), h * LANES:(h + 1) * LANES].astype(F32).T.astype(BF16))
    _flash_t(tiles, k_ref, vt_ref, n_lat=n_lat, n_ctx=n_ctx, latent=latent)
    for rb in range(row_blocks):
        for h in range(group):
            t = tiles[rb * group + h]
            o_ref[_row_block(rb), h * LANES:(h + 1) * LANES] = (t.acc[...] / t.l[...]).T.astype(o_ref.dtype)


def _attn_rows(n_lat, n_ctx, latent):
    if latent:
        assert n_lat % (LATENT_ROW_BLOCKS * ATTN_TQ) == 0
        return LATENT_ROW_BLOCKS, n_lat // (LATENT_ROW_BLOCKS * ATTN_TQ), 0
    assert n_ctx == ATTN_TQ
    return 1, 1, n_lat // ATTN_TQ


def _gqa_attention(q, k, vt, n_lat, n_ctx, latent):
    group = N_HEADS_A // N_KV_HEADS_A
    row_blocks, steps, first = _attn_rows(n_lat, n_ctx, latent)
    rows = row_blocks * ATTN_TQ
    return pl.pallas_call(
        functools.partial(_gqa_kernel, n_lat=n_lat, n_ctx=n_ctx, latent=latent, row_blocks=row_blocks),
        grid=(N_KV_HEADS_A, steps),
        in_specs=[pl.BlockSpec((rows, group * LANES), lambda g, i: (first + i, g))]
        + _kv_specs(n_lat + n_ctx, n_lat // ATTN_TK, LANES, 0),
        out_specs=pl.BlockSpec((rows, group * LANES), lambda g, i: (i, g)),
        out_shape=jax.ShapeDtypeStruct((steps * rows, QA_COLS), BF16),
        scratch_shapes=_attn_scratch(row_blocks * group),
        compiler_params=_params("arbitrary", "arbitrary"),
        name="gqa_attention" if latent else "gqa_attention_ctx",
    )(q, k, vt)


DIFF_HEADS_PER_STEP = 2


def _diff_kernel(q_ref, k_ref, vt_ref, lam_ref, gs_ref, o_ref, *scratch,
                 n_lat, n_ctx, lambda_init, latent, row_blocks):
    heads = DIFF_HEADS_PER_STEP
    tiles = _make_tiles(scratch, [h for _ in range(row_blocks) for h in range(heads) for _ in range(2)])
    lane = lax.broadcasted_iota(jnp.int32, (ATTN_TQ, LANES), 1)
    for rb in range(row_blocks):
        for h in range(heads):
            qh = q_ref[_row_block(rb), h * LANES:(h + 1) * LANES].astype(F32)
            zero = jnp.zeros_like(qh)
            first = (rb * heads + h) * 2
            tiles[first].qt[...] = jnp.where(lane < HEAD_DIM_B, qh, zero).T.astype(BF16)
            tiles[first + 1].qt[...] = jnp.where(lane >= HEAD_DIM_B, qh, zero).T.astype(BF16)
    _flash_t(tiles, k_ref, vt_ref, n_lat=n_lat, n_ctx=n_ctx, latent=latent)
    lv = lam_ref[...]
    lam = (jnp.exp(jnp.sum(lv[0:1] * lv[1:2], axis=-1, keepdims=True))
           - jnp.exp(jnp.sum(lv[2:3] * lv[3:4], axis=-1, keepdims=True)) + lambda_init)
    for rb in range(row_blocks):
        for h in range(heads):
            first = (rb * heads + h) * 2
            out1, out2 = (t.acc[...] / t.l[...] for t in tiles[first:first + 2])
            diff = (out1 - lam * out2).T
            o_ref[_row_block(rb), h * LANES:(h + 1) * LANES] = (
                _head_rmsnorm(diff, gs_ref[...]) * (1.0 - lambda_init)).astype(o_ref.dtype)


def _diff_attention(q, k, vt, lam_vecs, g_subln, lambda_init, n_lat, n_ctx, latent):
    width = DIFF_HEADS_PER_STEP * LANES
    assert QA_COLS % width == 0 and KA_COLS % width == 0
    row_blocks, steps, first = _attn_rows(n_lat, n_ctx, latent)
    rows = row_blocks * ATTN_TQ
    return pl.pallas_call(
        functools.partial(_diff_kernel, n_lat=n_lat, n_ctx=n_ctx, lambda_init=lambda_init,
                          latent=latent, row_blocks=row_blocks),
        grid=(N_HEADS_B // DIFF_HEADS_PER_STEP, steps),
        in_specs=[pl.BlockSpec((rows, width), lambda h, i: (first + i, QA_COLS // width + h))]
        + _kv_specs(n_lat + n_ctx, n_lat // ATTN_TK, width, KA_COLS // width)
        + [pl.BlockSpec((4, HEAD_DIM_B), lambda h, i: (0, 0)),
           pl.BlockSpec((1, LANES), lambda h, i: (0, 0))],
        out_specs=pl.BlockSpec((rows, width), lambda h, i: (i, h)),
        out_shape=jax.ShapeDtypeStruct((steps * rows, VB_COLS), BF16),
        scratch_shapes=_attn_scratch(row_blocks * DIFF_HEADS_PER_STEP * 2),
        compiler_params=_params("arbitrary", "arbitrary"),
        name="diff_attention" if latent else "diff_attention_ctx",
    )(q, k, vt, lam_vecs, g_subln.reshape(1, LANES))


def _dft_cos_sin(n):
    j = np.arange(n, dtype=np.float64)
    ang = 2.0 * np.pi * np.outer(j, j) / n
    return np.cos(ang), np.sin(ang)


def _channel_mix_weights(cs_ref, wf_ref, ab_ref):
    cs = cs_ref[...].astype(BF16)
    for g in range(N_FOURIER_GROUPS):
        ab_ref[g] = jnp.dot(cs, wf_ref[g].astype(BF16), preferred_element_type=F32).astype(BF16)


def _channel_mix(xr, xi, ab_ref, norm):
    outs = []
    for g in range(N_FOURIER_GROUPS):
        sl = slice(g * LANES, (g + 1) * LANES)
        lhs = jnp.concatenate([xr[:, sl], xi[:, sl]], axis=-1).astype(BF16)
        outs.append(jnp.dot(lhs, ab_ref[g], preferred_element_type=F32))
    return jnp.concatenate(outs, axis=-1) * norm


def _fourier_stage1_kernel(w_ref, u_ref, o_ref):
    o_ref[...] = jnp.dot(w_ref[...].astype(BF16), u_ref[...],
                         preferred_element_type=F32).astype(o_ref.dtype)


def _fourier_stage2_kernel(a_ref, tc_ref, ts_ref, w3_ref, cs_ref, wf_ref, o_ref, ab_ref, *, norm):
    @pl.when(pl.program_id(0) == 0)
    def _():
        _channel_mix_weights(cs_ref, wf_ref, ab_ref)

    ar = a_ref[0, 0].astype(F32)
    ai = a_ref[1, 0].astype(F32)
    tc, ts = tc_ref[0], ts_ref[0]
    br = ar * tc + ai * ts
    bi = ai * tc - ar * ts
    stack = jnp.concatenate([br, bi], axis=0).astype(BF16)
    x = jnp.dot(w3_ref[...].astype(BF16), stack, preferred_element_type=F32)
    half = x.shape[0] // 2
    o_ref[...] = _channel_mix(x[:half], x[half:], ab_ref, norm).astype(o_ref.dtype)


def _fourier_ctx_kernel(u_ref, w_ref, cs_ref, wf_ref, o_ref, ab_ref, *, norm):
    _channel_mix_weights(cs_ref, wf_ref, ab_ref)
    x = jnp.dot(w_ref[...].astype(BF16), u_ref[...], preferred_element_type=F32)
    half = x.shape[0] // 2
    o_ref[...] = _channel_mix(x[:half], x[half:], ab_ref, norm).astype(o_ref.dtype)


def _fourier_latent(u, n, w_fourier):
    assert u.shape[0] % LANES == 0
    n2 = LANES
    n1 = n // n2
    c1, s1 = _dft_cos_sin(n1)
    w1 = jnp.asarray(np.concatenate([c1, -s1], axis=0), F32)
    k1n2 = 2.0 * np.pi * np.outer(np.arange(n1), np.arange(n2)) / n
    tw_c = jnp.asarray(np.cos(k1n2)[:, :, None], F32)
    tw_s = jnp.asarray(np.sin(k1n2)[:, :, None], F32)
    c2, s2 = _dft_cos_sin(n2)
    w3 = jnp.asarray(np.block([[c2, s2], [-s2, c2]]), F32)
    cc, sc = _dft_cos_sin(FOURIER_GROUP_DIM)
    cs = jnp.asarray(np.concatenate([cc, sc], axis=0), F32)
    width = n2 * C_COLS
    cb = _largest_tile(width, (4096, 2048, 1024, 512))
    stage1 = pl.pallas_call(
        _fourier_stage1_kernel,
        grid=(width // cb,),
        in_specs=[pl.BlockSpec((2 * n1, n1), lambda j: (0, 0)),
                  pl.BlockSpec((n1, cb), lambda j: (0, j))],
        out_specs=pl.BlockSpec((2 * n1, cb), lambda j: (0, j)),
        out_shape=jax.ShapeDtypeStruct((2 * n1, width), BF16),
        compiler_params=_params("arbitrary"),
        name="fourier_stage1",
    )(w1, u.reshape(u.shape[0] // n2, width))
    a = stage1.reshape(2, n1, n2, C_COLS)
    norm = 1.0 / math.sqrt(n * FOURIER_GROUP_DIM)
    out = pl.pallas_call(
        functools.partial(_fourier_stage2_kernel, norm=norm),
        grid=(n1,),
        in_specs=[pl.BlockSpec((2, 1, n2, C_COLS), lambda k: (0, k, 0, 0)),
                  pl.BlockSpec((1, n2, 1), lambda k: (k, 0, 0)),
                  pl.BlockSpec((1, n2, 1), lambda k: (k, 0, 0)),
                  pl.BlockSpec((2 * n2, 2 * n2), lambda k: (0, 0)),
                  pl.BlockSpec((2 * FOURIER_GROUP_DIM, FOURIER_GROUP_DIM), lambda k: (0, 0)),
                  pl.BlockSpec((N_FOURIER_GROUPS, FOURIER_GROUP_DIM, FOURIER_GROUP_DIM),
                               lambda k: (0, 0, 0))],
        out_specs=pl.BlockSpec((n2, C_COLS), lambda k: (0, k)),
        out_shape=jax.ShapeDtypeStruct((n2, n1 * C_COLS), BF16),
        scratch_shapes=[pltpu.VMEM((N_FOURIER_GROUPS, 2 * FOURIER_GROUP_DIM, FOURIER_GROUP_DIM), BF16)],
        compiler_params=_params("arbitrary"),
        name="fourier_stage2",
    )(a, tw_c, tw_s, w3, cs, w_fourier)
    return out.reshape(n, C_COLS)


def _fourier_ctx(u_ctx, w_fourier):
    m = u_ctx.shape[0]
    c, s = _dft_cos_sin(m)
    w = jnp.asarray(np.concatenate([c, -s], axis=0), F32)
    cc, sc = _dft_cos_sin(FOURIER_GROUP_DIM)
    cs = jnp.asarray(np.concatenate([cc, sc], axis=0), F32)
    norm = 1.0 / math.sqrt(m * FOURIER_GROUP_DIM)
    return pl.pallas_call(
        functools.partial(_fourier_ctx_kernel, norm=norm),
        out_shape=jax.ShapeDtypeStruct((m, C_COLS), BF16),
        scratch_shapes=[pltpu.VMEM((N_FOURIER_GROUPS, 2 * FOURIER_GROUP_DIM, FOURIER_GROUP_DIM), BF16)],
        compiler_params=pltpu.CompilerParams(vmem_limit_bytes=VMEM_LIMIT),
        name="fourier_ctx",
    )(u_ctx, w, cs, w_fourier)


def _gated_residual(x, f, g, mod, gate_row):
    ms = jnp.mean(f * f, axis=-1, keepdims=True)
    return x + mod[gate_row:gate_row + 1] * (f * lax.rsqrt(ms + NORM_EPS) * g)


def _residual_kernel(x_ref, f_ref, g_ref, mod_ref, o_ref, *, gate_row):
    o_ref[...] = _gated_residual(x_ref[...], f_ref[...], g_ref[...], mod_ref[0], gate_row)


def _post_residual(x, f, g, mod, gate_row, n_rows, n_lat):
    d = x.shape[1]
    row = pl.BlockSpec((ROW_TILE, d), lambda i: (i, 0))
    return pl.pallas_call(
        functools.partial(_residual_kernel, gate_row=gate_row),
        grid=(n_rows // ROW_TILE,),
        in_specs=[row, row, pl.BlockSpec((1, d), lambda i: (0, 0)), _mod_spec(d, n_lat // ROW_TILE)],
        out_specs=row,
        out_shape=jax.ShapeDtypeStruct((n_rows, d), F32),
        compiler_params=_params("arbitrary"),
        name="post_residual",
    )(x, f, g.reshape(1, d), mod)


def _outproj_residual_kernel(*refs, n_pieces, gate_row):
    y_refs = refs[:n_pieces]
    w_ref, x_ref, g_ref, mod_ref, o_ref = refs[n_pieces:]
    y = jnp.concatenate([r[...] for r in y_refs], axis=-1)
    f = jnp.dot(y, w_ref[...], preferred_element_type=F32)
    o_ref[...] = _gated_residual(x_ref[...], f, g_ref[...], mod_ref[0], gate_row)


def _outproj_residual(y_pieces, w_bf16, x, g, mod, gate_row, n_rows, n_lat):
    d = x.shape[1]
    k = w_bf16.shape[0]
    assert sum(piece.shape[1] for piece in y_pieces) == k
    row = lambda w: pl.BlockSpec((ROW_TILE, w), lambda i: (i, 0))
    return pl.pallas_call(
        functools.partial(_outproj_residual_kernel, n_pieces=len(y_pieces), gate_row=gate_row),
        grid=(n_rows // ROW_TILE,),
        in_specs=[row(piece.shape[1]) for piece in y_pieces]
        + [pl.BlockSpec((k, d), lambda i: (0, 0)), row(d), pl.BlockSpec((1, d), lambda i: (0, 0)),
           _mod_spec(d, n_lat // ROW_TILE)],
        out_specs=row(d),
        out_shape=jax.ShapeDtypeStruct((n_rows, d), F32),
        compiler_params=_params("arbitrary"),
        name="out_proj_residual",
    )(*y_pieces, w_bf16, x, g.reshape(1, d), mod)


MOE_TM = 256
DISPATCH_TM = 2 * MOE_TM
R_IDX, R_W, R_RANK = 0, 2, 4


def _router_kernel(x_ref, g_ref, mod_ref, wr_ref, h_ref, route_ref, cnt_ref, carry_ref, *, shift_row):
    @pl.when(pl.program_id(0) == 0)
    def _():
        carry_ref[...] = jnp.zeros(carry_ref.shape, F32)

    h = _norm_modulate(x_ref[...], g_ref[...], mod_ref[0], shift_row)
    h_ref[...] = h
    logits = jnp.dot(h, wr_ref[...], preferred_element_type=F32, precision=lax.Precision.HIGHEST)
    rows = logits.shape[0]
    lane = lax.broadcasted_iota(jnp.int32, logits.shape, 1).astype(F32)
    neg = jnp.full(logits.shape, -jnp.inf, F32)
    l1 = jnp.where(lane < N_EXPERTS, logits, neg)
    m1 = jnp.max(l1, axis=-1, keepdims=True)
    i1 = jnp.min(jnp.where(l1 == m1, lane, float(LANES)), axis=-1, keepdims=True)
    l2 = jnp.where(lane == i1, neg, l1)
    m2 = jnp.max(l2, axis=-1, keepdims=True)
    i2 = jnp.min(jnp.where(l2 == m2, lane, float(LANES)), axis=-1, keepdims=True)
    e2 = jnp.exp(m2 - m1)
    w1 = 1.0 / (1.0 + e2)
    w2 = e2 / (1.0 + e2)
    hit1 = lane == i1
    hit2 = lane == i2
    onehot = jnp.logical_or(hit1, hit2).astype(BF16)
    r_i = lax.broadcasted_iota(jnp.int32, (rows, rows), 0)
    c_i = lax.broadcasted_iota(jnp.int32, (rows, rows), 1)
    before = (c_i < r_i).astype(BF16)
    base = carry_ref[...] + jnp.dot(before, onehot, preferred_element_type=F32)
    rank1 = jnp.sum(jnp.where(hit1, base, 0.0), axis=-1, keepdims=True)
    rank2 = jnp.sum(jnp.where(hit2, base, 0.0), axis=-1, keepdims=True)
    carry_ref[...] = carry_ref[...] + jnp.sum(onehot.astype(F32), axis=0, keepdims=True)
    rec = jnp.zeros(logits.shape, F32)
    for off, val in ((R_IDX, i1), (R_IDX + 1, i2), (R_W, w1), (R_W + 1, w2),
                     (R_RANK, rank1), (R_RANK + 1, rank2)):
        rec = jnp.where(lane == off, val, rec)
    route_ref[...] = rec
    cnt_ref[...] = jnp.broadcast_to(carry_ref[...], cnt_ref.shape)


def _router(x, g, mod, w_router, n_lat):
    d = x.shape[1]
    wr = jnp.zeros((d, LANES), F32).at[:, :N_EXPERTS].set(w_router)
    return pl.pallas_call(
        functools.partial(_router_kernel, shift_row=3),
        grid=(n_lat // ROW_TILE,),
        in_specs=[pl.BlockSpec((ROW_TILE, d), lambda i: (i, 0)),
                  pl.BlockSpec((1, d), lambda i: (0, 0)),
                  pl.BlockSpec((1, N_MOD, d), lambda i: (0, 0, 0)),
                  pl.BlockSpec((d, LANES), lambda i: (0, 0))],
        out_specs=[pl.BlockSpec((ROW_TILE, d), lambda i: (i, 0)),
                   pl.BlockSpec((ROW_TILE, LANES), lambda i: (i, 0)),
                   pl.BlockSpec((8, LANES), lambda i: (0, 0))],
        out_shape=[jax.ShapeDtypeStruct((n_lat, d), F32),
                   jax.ShapeDtypeStruct((n_lat, LANES), F32),
                   jax.ShapeDtypeStruct((8, LANES), F32)],
        scratch_shapes=[pltpu.VMEM((1, LANES), F32)],
        compiler_params=_params("arbitrary"),
        name="router",
    )(x, g.reshape(1, d), mod, wr)


def _row_copy(src_hbm, src_row, dst_ref, dst_row, sem):
    return pltpu.make_async_copy(src_hbm.at[pl.ds(src_row, 1)], dst_ref.at[pl.ds(dst_row, 1)], sem)


def _wait_rows(src_hbm, dst_ref, sem):
    pltpu.make_async_copy(src_hbm.at[pl.ds(0, dst_ref.shape[0])], dst_ref, sem).wait()


ISSUE_UNROLL = 8


def _dispatch_kernel(src_ref, nv_ref, h_hbm, o_ref, buf_ref, sems):
    i = pl.program_id(0)
    n_valid = nv_ref[0]
    slot = lax.rem(i, 2)

    def issue(tile, slot):
        base = tile * DISPATCH_TM

        def body(r, carry):
            _row_copy(h_hbm, src_ref[base + r], buf_ref.at[slot], r, sems.at[slot]).start()
            return carry
        lax.fori_loop(0, DISPATCH_TM, body, 0, unroll=ISSUE_UNROLL)

    @pl.when(jnp.logical_and(i == 0, n_valid > 0))
    def _():
        issue(0, 0)

    @pl.when(i + 1 < n_valid)
    def _():
        issue(i + 1, 1 - slot)

    @pl.when(i < n_valid)
    def _():
        _wait_rows(h_hbm, buf_ref.at[slot], sems.at[slot])
        o_ref[...] = buf_ref[slot].astype(o_ref.dtype)

    @pl.when(i >= n_valid)
    def _():
        o_ref[...] = jnp.zeros(o_ref.shape, o_ref.dtype)


def _dispatch(src_token, n_valid, h):
    d = h.shape[1]
    n_slots = src_token.shape[0]
    grid_spec = pltpu.PrefetchScalarGridSpec(
        num_scalar_prefetch=2,
        grid=(n_slots // DISPATCH_TM,),
        in_specs=[pl.BlockSpec(memory_space=pl.ANY)],
        out_specs=pl.BlockSpec((DISPATCH_TM, d), lambda i, src, nv: (i, 0)),
        scratch_shapes=[pltpu.VMEM((2, DISPATCH_TM, d), F32), pltpu.SemaphoreType.DMA((2,))],
    )
    return pl.pallas_call(
        _dispatch_kernel,
        grid_spec=grid_spec,
        out_shape=jax.ShapeDtypeStruct((n_slots, d), BF16),
        compiler_params=_params("arbitrary"),
        name="moe_dispatch",
    )(src_token, n_valid, h)


def _combine_kernel(dest_ref, y_hbm, route_ref, x_ref, g_ref, mod_ref, o_ref, buf_ref, sems, *, gate_row):
    i = pl.program_id(0)
    slot = lax.rem(i, 2)

    def issue(tile, slot):
        base = tile * ROW_TILE

        def body(r, carry):
            for k in range(TOP_K):
                _row_copy(y_hbm, dest_ref[TOP_K * (base + r) + k], buf_ref.at[slot], k * ROW_TILE + r,
                          sems.at[slot]).start()
            return carry
        lax.fori_loop(0, ROW_TILE, body, 0, unroll=ISSUE_UNROLL)

    @pl.when(i == 0)
    def _():
        issue(0, 0)

    @pl.when(i + 1 < pl.num_programs(0))
    def _():
        issue(i + 1, 1 - slot)

    _wait_rows(y_hbm, buf_ref.at[slot], sems.at[slot])
    route = route_ref[...]
    f = (route[:, R_W:R_W + 1] * buf_ref[slot, 0:ROW_TILE, :]
         + route[:, R_W + 1:R_W + 2] * buf_ref[slot, ROW_TILE:TOP_K * ROW_TILE, :])
    o_ref[...] = _gated_residual(x_ref[...], f, g_ref[...], mod_ref[0], gate_row)


def _combine_residual(dest_flat, y_sorted, route, x, g, mod, gate_row, n_lat):
    d = x.shape[1]
    row = lambda w: pl.BlockSpec((ROW_TILE, w), lambda i, dest: (i, 0))
    grid_spec = pltpu.PrefetchScalarGridSpec(
        num_scalar_prefetch=1,
        grid=(n_lat // ROW_TILE,),
        in_specs=[pl.BlockSpec(memory_space=pl.ANY), row(LANES), row(d),
                  pl.BlockSpec((1, d), lambda i, dest: (0, 0)),
                  pl.BlockSpec((1, N_MOD, d), lambda i, dest: (0, 0, 0))],
        out_specs=row(d),
        scratch_shapes=[pltpu.VMEM((2, TOP_K * ROW_TILE, d), F32), pltpu.SemaphoreType.DMA((2,))],
    )
    return pl.pallas_call(
        functools.partial(_combine_kernel, gate_row=gate_row),
        grid_spec=grid_spec,
        out_shape=jax.ShapeDtypeStruct((n_lat, d), F32),
        compiler_params=_params("arbitrary"),
        name="moe_combine",
    )(dest_flat, y_sorted, route, x, g.reshape(1, d), mod)


def _tile_meta(starts, counts, ends, n_slots, tm):
    tile_start = jnp.arange(n_slots // tm, dtype=jnp.int32) * tm
    group = jnp.minimum(jnp.sum((tile_start[:, None] >= ends[None, :]).astype(jnp.int32), axis=1),
                        N_EXPERTS - 1)
    rows = jnp.clip(starts[group] + counts[group] - tile_start, 0, tm)
    return group, rows


def _moe_layer(x, g_pre, g_post, mod, w_router, w_gate, w_up, w_down, n_lat):
    d_ff = w_gate.shape[-1]
    h, route, cnt = _router(x, g_pre, mod, w_router, n_lat)
    experts = route[:, R_IDX:R_IDX + TOP_K].astype(jnp.int32)
    ranks = route[:, R_RANK:R_RANK + TOP_K].astype(jnp.int32)
    counts = cnt[0, :N_EXPERTS].astype(jnp.int32)
    padded = ((counts + MOE_TM - 1) // MOE_TM) * MOE_TM
    ends = jnp.cumsum(padded)
    starts = ends - padded
    dest = (starts[experts] + ranks).reshape(-1)
    n_slots = TOP_K * n_lat + N_EXPERTS * MOE_TM
    assert n_slots % DISPATCH_TM == 0
    n_valid = ((ends[-1:] + DISPATCH_TM - 1) // DISPATCH_TM).astype(jnp.int32)

    token = jnp.repeat(jnp.arange(n_lat, dtype=jnp.int32), TOP_K)
    src_token = jnp.zeros((n_slots,), jnp.int32).at[dest].set(token)
    hs = _dispatch(src_token, n_valid, h)
    group, rows = _tile_meta(starts, counts, ends, n_slots, MOE_TM)
    act = _gmm(hs, [w_gate, w_up], group, rows, tm=MOE_TM,
               tn=_largest_tile(d_ff, (1024, 512, 256, 128)), out_dtype=BF16, swiglu=True, name="moe_gate_up")
    y = _gmm(act, [w_down], group, rows, tm=MOE_TM,
             tn=_largest_tile(x.shape[1], (512, 256, 128)), out_dtype=F32, name="moe_down")
    return _combine_residual(dest, y, route, x, g_post, mod, 5, n_lat)


def kernel(x, c, ctx, c_ctx, w_mod, b_mod, g_pre_mix, g_post_mix, g_pre_ffn, g_post_ffn, w_in, g_q_a, g_k_a, lambda_q1, lambda_k1, lambda_q2, lambda_k2, g_subln_b, w_fourier, w_out, w_gate_dense, w_up_dense, w_down_dense, w_router, w_gate_moe, w_up_moe, w_down_moe):
    batch, n_lat, d = x.shape
    n_ctx = ctx.shape[1]
    depth = w_mod.shape[0]
    assert batch == 1 and n_ctx == ROW_TILE and n_lat % max(2 * ATTN_TK, MOE_TM, GRID_W) == 0
    assert depth == 2, "layer 0 dense, layer 1 (last) routed experts"
    t = n_lat + n_ctx

    cond_rows = jnp.concatenate([c, c_ctx[None, :], jnp.zeros((6, d), F32)], axis=0)
    mod_all = _modulation(cond_rows, w_mod, b_mod)
    rope_a = _rope_tables(n_lat, n_ctx, HEAD_DIM_A, 1)
    rope_b = _rope_tables(n_lat, n_ctx, HEAD_DIM_B, 2)

    xs = jnp.concatenate([x[0], ctx[0]], axis=0)
    for layer in range(depth):
        last = layer == depth - 1
        n_out = n_lat if last else t
        mod = mod_all[layer, :2].reshape(2, N_MOD, d)
        lambda_init = 0.8 - 0.6 * math.exp(-0.3 * layer)

        q, k, u, vt = _mixer_front(xs, g_pre_mix[layer], mod, w_in[layer].astype(BF16),
                                   g_q_a[layer], g_k_a[layer], rope_a, rope_b, n_lat)
        lam_vecs = jnp.stack([lambda_q1[layer], lambda_k1[layer], lambda_q2[layer], lambda_k2[layer]])
        mixers = [[_gqa_attention(q, k, vt, n_lat, n_ctx, latent),
                   _diff_attention(q, k, vt, lam_vecs, g_subln_b[layer], lambda_init, n_lat, n_ctx, latent),
                   _fourier_latent(u, n_lat, w_fourier[layer]) if latent
                   else _fourier_ctx(u[n_lat:], w_fourier[layer])]
                  for latent in ((True,) if last else (True, False))]
        y_a, y_b, y_c = (jnp.concatenate(parts, axis=0) for parts in zip(*mixers))
        xs = _outproj_residual([y_a, y_b, y_c], w_out[layer].astype(BF16), xs, g_post_mix[layer], mod, 2,
                               n_out, n_lat)

        j = layer // 2
        if layer % 2 == 0:
            h = _prenorm(xs, g_pre_ffn[layer], mod, 3, n_out, n_lat)
            act = _dense_matmul(h, w_gate_dense[j], w2=w_up_dense[j], swiglu=True, tn=512,
                                out_dtype=BF16, name="ffn_gate_up")
            f = _dense_matmul(act, w_down_dense[j], tn=512, tiles=(ROW_TILE,), out_dtype=F32, name="ffn_down")
            xs = _post_residual(xs, f, g_post_ffn[layer], mod, 5, n_out, n_lat)
        else:
            xs = _moe_layer(xs, g_pre_ffn[layer], g_post_ffn[layer], mod, w_router[j],
                            w_gate_moe[j], w_up_moe[j], w_down_moe[j], n_lat)
    return xs[:n_lat][None]
```

```python
import functools
import math

import numpy as np
import jax
import jax.numpy as jnp
from jax import lax
from jax.experimental import pallas as pl
from jax.experimental.pallas import tpu as pltpu

F32 = jnp.float32
BF16 = jnp.bfloat16

GRID_W = 64
ROPE_THETA = 10000.0
NORM_EPS = 1e-6
N_HEADS_A, N_KV_HEADS_A, HEAD_DIM_A = 8, 2, 128
N_HEADS_B, HEAD_DIM_B = 4, 64
VALUE_DIM_B = 2 * HEAD_DIM_B
N_FOURIER_GROUPS, FOURIER_GROUP_DIM = 4, 128
QA_COLS = N_HEADS_A * HEAD_DIM_A
QB_COLS = N_HEADS_B * 2 * HEAD_DIM_B
C_COLS = N_FOURIER_GROUPS * FOURIER_GROUP_DIM
KA_COLS = N_KV_HEADS_A * HEAD_DIM_A
VA_COLS = KA_COLS
KB_COLS = QB_COLS
VB_COLS = N_HEADS_B * VALUE_DIM_B
KV_START = QA_COLS + QB_COLS + C_COLS
IN_COLS = KV_START + KA_COLS + VA_COLS + KB_COLS + VB_COLS
N_EXPERTS, TOP_K = 8, 2
N_MOD = 6

LANES = 128
V7X_VMEM_BYTES = 64 * 1024 * 1024
VMEM_LIMIT = 52 * 1024 * 1024

ROW_TILE = 256
NEG_BIG = -1e30
LOG2_E = math.log2(math.e)


def _params(*sem):
    return pltpu.CompilerParams(dimension_semantics=sem, vmem_limit_bytes=VMEM_LIMIT)


def _silu(x):
    return x / (1.0 + jnp.exp(-x))


def _largest_tile(n, candidates):
    for c in candidates:
        if n % c == 0:
            return c
    raise ValueError(f"no tile in {candidates} divides {n}")


def _mod_kernel(c_ref, w_ref, b_ref, o_ref):
    cond = _silu(c_ref[...])
    o_ref[0] = jnp.dot(cond.astype(BF16), w_ref[0].astype(BF16),
                       preferred_element_type=F32) + b_ref[0]


def _modulation(cond_rows, w_mod, b_mod):
    depth, d, n6 = w_mod.shape
    tn = _largest_tile(n6, (1024, 512, 256, 128))
    return pl.pallas_call(
        _mod_kernel,
        grid=(depth, n6 // tn),
        in_specs=[pl.BlockSpec((8, d), lambda l, j: (0, 0)),
                  pl.BlockSpec((1, d, tn), lambda l, j: (l, 0, j)),
                  pl.BlockSpec((1, 1, tn), lambda l, j: (l, 0, j))],
        out_specs=pl.BlockSpec((1, 8, tn), lambda l, j: (l, 0, j)),
        out_shape=jax.ShapeDtypeStruct((depth, 8, n6), F32),
        compiler_params=_params("arbitrary", "arbitrary"),
        name="modulation",
    )(cond_rows, w_mod, b_mod.reshape(depth, 1, n6))


def _norm_modulate(x, g, mod, shift_row):
    ms = jnp.mean(x * x, axis=-1, keepdims=True)
    y = x * lax.rsqrt(ms + NORM_EPS) * g
    return y * (1.0 + mod[shift_row + 1:shift_row + 2]) + mod[shift_row:shift_row + 1]


def _prenorm_kernel(x_ref, g_ref, mod_ref, o_ref, *, shift_row):
    h = _norm_modulate(x_ref[...], g_ref[...], mod_ref[0], shift_row)
    o_ref[...] = h.astype(o_ref.dtype)


def _mod_spec(d, n_lat_tiles):
    return pl.BlockSpec((1, N_MOD, d), lambda i: ((i >= n_lat_tiles).astype(jnp.int32), 0, 0))


def _prenorm(x, g, mod, shift_row, n_rows, n_lat):
    d = x.shape[1]
    return pl.pallas_call(
        functools.partial(_prenorm_kernel, shift_row=shift_row),
        grid=(n_rows // ROW_TILE,),
        in_specs=[pl.BlockSpec((ROW_TILE, d), lambda i: (i, 0)),
                  pl.BlockSpec((1, d), lambda i: (0, 0)),
                  _mod_spec(d, n_lat // ROW_TILE)],
        out_specs=pl.BlockSpec((ROW_TILE, d), lambda i: (i, 0)),
        out_shape=jax.ShapeDtypeStruct((n_rows, d), BF16),
        compiler_params=_params("arbitrary"),
        name="prenorm",
    )(x, g.reshape(1, d), mod)


def _matmul_act(x, wb_ref, swiglu):
    if swiglu:
        gate = jnp.dot(x, wb_ref[0], preferred_element_type=F32)
        up = jnp.dot(x, wb_ref[1], preferred_element_type=F32)
        return _silu(gate) * up
    return jnp.dot(x, wb_ref[0], preferred_element_type=F32)


def _dense_mm_kernel(lhs_ref, *refs, n_w, swiglu):
    w_refs, (o_ref, wb_ref) = refs[:n_w], refs[n_w:]

    @pl.when(pl.program_id(1) == 0)
    def _():
        for i in range(n_w):
            wb_ref[i] = w_refs[i][...].astype(BF16)

    o_ref[...] = _matmul_act(lhs_ref[...], wb_ref, swiglu).astype(o_ref.dtype)


def _expert_mm_kernel(start_ref, tiles_ref, lhs_hbm, *refs, n_w, n_experts, swiglu, tm, tn):
    w_refs = refs[:n_w]
    out_hbm, wb_ref, lhs_buf, out_buf, in_sems, out_sems = refs[n_w:]
    e = pl.program_id(1)
    row0 = start_ref[e]
    cnt = tiles_ref[e]
    col0 = pl.multiple_of(pl.program_id(0) * tn, tn)

    def in_copy(i, slot):
        r = pl.multiple_of(row0 + i * tm, tm)
        return pltpu.make_async_copy(lhs_hbm.at[pl.ds(r, tm)], lhs_buf.at[slot], in_sems.at[slot])

    def out_copy(i, slot):
        r = pl.multiple_of(row0 + i * tm, tm)
        return pltpu.make_async_copy(out_buf.at[slot], out_hbm.at[pl.ds(r, tm), pl.ds(col0, tn)],
                                     out_sems.at[slot])

    @pl.when(e < n_experts)
    def _():
        for i in range(n_w):
            wb_ref[i] = w_refs[i][0].astype(BF16)

        @pl.when(cnt > 0)
        def _():
            in_copy(0, 0).start()

        def body(i, carry):
            slot = lax.rem(i, 2)

            @pl.when(i + 1 < cnt)
            def _():
                in_copy(i + 1, 1 - slot).start()

            in_copy(i, slot).wait()

            @pl.when(i >= 2)
            def _():
                out_copy(i - 2, slot).wait()

            out_buf[slot] = _matmul_act(lhs_buf[slot], wb_ref, swiglu).astype(out_buf.dtype)
            out_copy(i, slot).start()
            return carry
        lax.fori_loop(0, cnt, body, 0)

        @pl.when(cnt >= 2)
        def _():
            out_copy(cnt - 2, lax.rem(cnt, 2)).wait()

        @pl.when(cnt >= 1)
        def _():
            out_copy(cnt - 1, lax.rem(cnt + 1, 2)).wait()

    @pl.when(e == n_experts)
    def _():
        out_buf[0] = jnp.zeros(out_buf.shape[1:], out_buf.dtype)

        def body(i, carry):
            out_copy(i, 0).start()
            out_copy(i, 0).wait()
            return carry
        lax.fori_loop(0, cnt, body, 0)


def _expert_matmul(lhs, weights, seg_start, seg_tiles, *, tm, tn, out_dtype, swiglu=False, name="expert_mm"):
    n_experts, k, nw = weights[0].shape
    n_w = len(weights)
    grid_spec = pltpu.PrefetchScalarGridSpec(
        num_scalar_prefetch=2,
        grid=(nw // tn, n_experts + 1),
        in_specs=[pl.BlockSpec(memory_space=pl.ANY)]
        + [pl.BlockSpec((1, k, tn), lambda n, e, st, ti: (jnp.minimum(e, n_experts - 1), 0, n))
           for _ in range(n_w)],
        out_specs=pl.BlockSpec(memory_space=pl.ANY),
        scratch_shapes=[pltpu.VMEM((n_w, k, tn), BF16), pltpu.VMEM((2, tm, k), BF16),
                        pltpu.VMEM((2, tm, tn), out_dtype),
                        pltpu.SemaphoreType.DMA((2,)), pltpu.SemaphoreType.DMA((2,))],
    )
    return pl.pallas_call(
        functools.partial(_expert_mm_kernel, n_w=n_w, n_experts=n_experts, swiglu=swiglu, tm=tm, tn=tn),
        grid_spec=grid_spec,
        out_shape=jax.ShapeDtypeStruct((lhs.shape[0], nw), out_dtype),
        compiler_params=_params("arbitrary", "arbitrary"),
        name=name,
    )(seg_start, seg_tiles, lhs, *weights)


def _dense_matmul(lhs, w, *, tn, out_dtype, tiles=(1024, 768, 512, 256), swiglu=False, w2=None,
                  name="matmul"):
    n_rows, k = lhs.shape
    nw = w.shape[1]
    tm = _largest_tile(n_rows, tiles)
    weights = [w] if w2 is None else [w, w2]
    return pl.pallas_call(
        functools.partial(_dense_mm_kernel, n_w=len(weights), swiglu=swiglu),
        grid=(nw // tn, n_rows // tm),
        in_specs=[pl.BlockSpec((tm, k), lambda n, m: (m, 0))]
        + [pl.BlockSpec((k, tn), lambda n, m: (0, n)) for _ in weights],
        out_specs=pl.BlockSpec((tm, tn), lambda n, m: (m, n)),
        out_shape=jax.ShapeDtypeStruct((n_rows, nw), out_dtype),
        scratch_shapes=[pltpu.VMEM((len(weights), k, tn), BF16)],
        compiler_params=_params("arbitrary", "arbitrary"),
        name=name,
    )(lhs, *weights)


def _rope_tables(n_lat, n_ctx, head_dim, reps):
    n_rows_grid = n_lat // GRID_W
    rows = jnp.repeat(jnp.arange(n_rows_grid), GRID_W).astype(F32)
    cols = jnp.tile(jnp.arange(GRID_W), n_rows_grid).astype(F32)
    axis_dim = head_dim // 2
    freqs = ROPE_THETA ** (-jnp.arange(0, axis_dim, 2, dtype=F32) / axis_dim)
    ra, ca = rows[:, None] * freqs, cols[:, None] * freqs
    rc, rs, cc, cs = jnp.cos(ra), jnp.sin(ra), jnp.cos(ca), jnp.sin(ca)
    z = jnp.zeros_like(rs)
    cos = jnp.concatenate([rc, rc, cc, cc], axis=-1)
    sin_up = jnp.concatenate([-rs, z, -cs, z], axis=-1)
    sin_dn = jnp.concatenate([z, rs, z, cs], axis=-1)
    pad = lambda t, v: jnp.concatenate(
        [jnp.tile(t, (1, reps)), jnp.full((n_ctx, LANES), v, F32)], axis=0)
    return pad(cos, 1.0), pad(sin_up, 0.0), pad(sin_dn, 0.0)


def _rope(x, cos, sin_up, sin_dn, quarter):
    return (x * cos + pltpu.roll(x, LANES - quarter, 1) * sin_up
            + pltpu.roll(x, quarter, 1) * sin_dn)


def _head_rmsnorm(x, g):
    return x * lax.rsqrt(jnp.mean(x * x, axis=-1, keepdims=True) + NORM_EPS) * g


def _qkpost_kernel(p_ref, gq_ref, gk_ref, ca_ref, ua_ref, da_ref, cb_ref, ub_ref, db_ref,
                   q_ref, k_ref, u_ref, vt_ref):
    def col(c):
        return p_ref[:, c * LANES:(c + 1) * LANES]

    def put(ref, c, x):
        ref[:, c * LANES:(c + 1) * LANES] = x.astype(BF16)

    is_tail = pl.program_id(0) == pl.num_programs(0) - 1

    def put_t(c, x):
        xt = x.T
        vt_ref[0, c * LANES:(c + 1) * LANES, :] = jnp.where(is_tail, jnp.zeros_like(xt), xt).astype(BF16)

    rope_a = lambda x: _rope(x, ca_ref[...], ua_ref[...], da_ref[...], HEAD_DIM_A // 4)
    rope_b = lambda x: _rope(x, cb_ref[...], ub_ref[...], db_ref[...], HEAD_DIM_B // 4)
    scale_a, scale_b = LOG2_E * HEAD_DIM_A ** -0.5, LOG2_E * HEAD_DIM_B ** -0.5
    c0 = 0
    for h in range(N_HEADS_A):
        put(q_ref, h, rope_a(_head_rmsnorm(col(c0 + h), gq_ref[...])) * scale_a)
    c0 += N_HEADS_A
    for h in range(N_HEADS_B):
        put(q_ref, N_HEADS_A + h, rope_b(col(c0 + h)) * scale_b)
    c0 += N_HEADS_B
    for g in range(N_FOURIER_GROUPS):
        put(u_ref, g, col(c0 + g))
    c0 += N_FOURIER_GROUPS
    for h in range(N_KV_HEADS_A):
        put(k_ref, h, rope_a(_head_rmsnorm(col(c0 + h), gk_ref[...])))
    c0 += N_KV_HEADS_A
    for h in range(N_KV_HEADS_A):
        put_t(h, col(c0 + h))
    c0 += N_KV_HEADS_A
    for h in range(N_HEADS_B):
        put(k_ref, N_KV_HEADS_A + h, rope_b(col(c0 + h)))
    c0 += N_HEADS_B
    for h in range(N_HEADS_B):
        put_t(N_KV_HEADS_A + h, col(c0 + h))


Q_COLS = QA_COLS + QB_COLS
K_COLS = KA_COLS + KB_COLS
V_COLS = VA_COLS + VB_COLS


def _mixer_front_kernel(x_ref, g_ref, mod_ref, w_ref, *rest):
    p_ref = rest[-1]
    h = _norm_modulate(x_ref[...], g_ref[...], mod_ref[0], 0)
    p_ref[...] = jnp.dot(h.astype(BF16), w_ref[...], preferred_element_type=F32)
    _qkpost_kernel(p_ref, *rest[:-1])


def _mixer_front(x, g, mod, w_in_bf16, g_q, g_k, rope_a, rope_b, n_lat):
    t, d = x.shape
    n_tiles = t // ROW_TILE
    n_lat_tiles = n_lat // ROW_TILE
    tiles_per_chunk = ATTN_TK // ROW_TILE
    assert tiles_per_chunk == 2 and (n_tiles + 1) % tiles_per_chunk == 0
    row = lambda w: pl.BlockSpec((ROW_TILE, w), lambda i: (jnp.minimum(i, n_tiles - 1), 0))
    vec = lambda w: pl.BlockSpec((1, w), lambda i: (0, 0))
    vt_spec = pl.BlockSpec((1, V_COLS, ROW_TILE), lambda i: (i // tiles_per_chunk, 0, i % tiles_per_chunk))
    return pl.pallas_call(
        _mixer_front_kernel,
        grid=(n_tiles + 1,),
        in_specs=[row(d), vec(d), _mod_spec(d, n_lat_tiles), pl.BlockSpec((d, IN_COLS), lambda i: (0, 0)),
                  vec(LANES), vec(LANES)] + [row(LANES)] * 6,
        out_specs=[row(Q_COLS), row(K_COLS), row(C_COLS), vt_spec],
        out_shape=[jax.ShapeDtypeStruct((t, Q_COLS), BF16), jax.ShapeDtypeStruct((t, K_COLS), BF16),
                   jax.ShapeDtypeStruct((t, C_COLS), BF16),
                   jax.ShapeDtypeStruct((n_lat // ATTN_TK + 1, V_COLS, ATTN_TK), BF16)],
        scratch_shapes=[pltpu.VMEM((ROW_TILE, IN_COLS), F32)],
        compiler_params=_params("arbitrary"),
        name="mixer_front",
    )(x, g.reshape(1, d), mod, w_in_bf16, g_q.reshape(1, LANES), g_k.reshape(1, LANES), *rope_a, *rope_b)


ATTN_TQ = 256
ATTN_TK = 512


REDUCE_ROWS = 32


def _stream_reduce(op, x):
    rows = x.shape[0]
    step = REDUCE_ROWS if rows % REDUCE_ROWS == 0 else rows
    acc = x[:step]
    for r in range(step, rows, step):
        acc = op(acc, x[r:r + step])
    red = jnp.max if op is jnp.maximum else jnp.sum
    return red(acc, axis=0, keepdims=True)


ONES_ROWS = 16


class _Tile:
    def __init__(self, refs, kv):
        self.qt, self.m, self.l, self.acc, s_a, c_a, s_b, c_b = refs
        self.stage = ((s_a, c_a), (s_b, c_b))
        self.kv = slice(kv * LANES, (kv + 1) * LANES)


def _attn_scratch(n_tiles):
    stat = pltpu.VMEM((1, ATTN_TQ), F32)
    score = pltpu.VMEM((ATTN_TK, ATTN_TQ), F32)
    per_tile = [pltpu.VMEM((LANES, ATTN_TQ), BF16), stat, stat, pltpu.VMEM((LANES, ATTN_TQ), F32),
                score, stat, score, stat]
    return per_tile * n_tiles


def _make_tiles(scratch, kv_of_tile):
    n = len(scratch) // len(kv_of_tile)
    return [_Tile(scratch[n * i:n * (i + 1)], kv) for i, kv in enumerate(kv_of_tile)]


def _flash_t(tiles, k_ref, vt_ref, *, n_lat, n_ctx, latent):
    n_chunks = n_lat // ATTN_TK
    for t in tiles:
        t.m[...] = jnp.full(t.m.shape, NEG_BIG, F32)
        t.l[...] = jnp.zeros(t.l.shape, F32)
        t.acc[...] = jnp.zeros(t.acc.shape, F32)

    def scores(t, k, stage, rows):
        s = jnp.dot(k, t.qt[...], preferred_element_type=F32)
        s_ref, c_ref = t.stage[stage]
        s_ref[0:rows, :] = s
        c_ref[...] = _stream_reduce(jnp.maximum, s)

    def scores_lat(j, stage):
        off = pl.multiple_of(j * ATTN_TK, ATTN_TK)
        for t in tiles:
            scores(t, k_ref[pl.ds(off, ATTN_TK), t.kv], stage, ATTN_TK)

    def scores_ctx():
        for t in tiles:
            scores(t, k_ref[n_lat:n_lat + n_ctx, t.kv], 0, n_ctx)

    def consume(t, vt, stage, rows):
        s_ref, c_ref = t.stage[stage]
        m_old = t.m[...]
        m_new = jnp.maximum(m_old, c_ref[...])
        alpha = jnp.exp2(m_old - m_new)
        p = jnp.exp2(s_ref[0:rows, :] - m_new).astype(BF16)
        vt_ones = jnp.concatenate([vt, jnp.ones((ONES_ROWS, rows), BF16)], axis=0)
        pv = jnp.dot(vt_ones, p, preferred_element_type=F32)
        t.l[...] = alpha * t.l[...] + pv[LANES:LANES + 1]
        t.acc[...] = alpha * t.acc[...] + pv[:LANES]
        t.m[...] = m_new

    def consume_lat(j, stage):
        for t in tiles:
            consume(t, vt_ref[j, t.kv, :], stage, ATTN_TK)

    if latent:
        scores_lat(0, 0)

        def body(i, carry):
            j = 2 * i
            scores_lat(j + 1, 1)
            consume_lat(j, 0)
            scores_lat(j + 2, 0)
            consume_lat(j + 1, 1)
            return carry
        lax.fori_loop(0, n_chunks // 2 - 1, body, 0)
        scores_lat(n_chunks - 1, 1)
        consume_lat(n_chunks - 2, 0)
        scores_ctx()
        consume_lat(n_chunks - 1, 1)
    else:
        scores_ctx()
    for t in tiles:
        consume(t, vt_ref[n_chunks, t.kv, 0:n_ctx], 0, n_ctx)


def _kv_specs(t, n_chunks, width, first_blk):
    return [pl.BlockSpec((t, width), lambda g, i: (0, first_blk + g)),
            pl.BlockSpec((n_chunks + 1, width, ATTN_TK), lambda g, i: (0, first_blk + g, 0))]


LATENT_ROW_BLOCKS = 2


def _row_block(rb):
    return slice(rb * ATTN_TQ, (rb + 1) * ATTN_TQ)


def _gqa_kernel(q_ref, k_ref, vt_ref, o_ref, *scratch, n_lat, n_ctx, latent, row_blocks):
    group = N_HEADS_A // N_KV_HEADS_A
    tiles = _make_tiles(scratch, [0] * (row_blocks * group))
    for rb in range(row_blocks):
        for h in range(group):
            tiles[rb * group + h].qt[...] = (
                q_ref[_row_block(rb), h * LANES:(h + 1) * LANES].astype(F32).T.astype(BF16))
    _flash_t(tiles, k_ref, vt_ref, n_lat=n_lat, n_ctx=n_ctx, latent=latent)
    for rb in range(row_blocks):
        for h in range(group):
            t = tiles[rb * group + h]
            o_ref[_row_block(rb), h * LANES:(h + 1) * LANES] = (t.acc[...] / t.l[...]).T.astype(o_ref.dtype)


def _attn_rows(n_lat, n_ctx, latent):
    if latent:
        assert n_lat % (LATENT_ROW_BLOCKS * ATTN_TQ) == 0
        return LATENT_ROW_BLOCKS, n_lat // (LATENT_ROW_BLOCKS * ATTN_TQ), 0
    assert n_ctx == ATTN_TQ
    return 1, 1, n_lat // ATTN_TQ


def _gqa_attention(q, k, vt, n_lat, n_ctx, latent):
    group = N_HEADS_A // N_KV_HEADS_A
    row_blocks, steps, first = _attn_rows(n_lat, n_ctx, latent)
    rows = row_blocks * ATTN_TQ
    return pl.pallas_call(
        functools.partial(_gqa_kernel, n_lat=n_lat, n_ctx=n_ctx, latent=latent, row_blocks=row_blocks),
        grid=(N_KV_HEADS_A, steps),
        in_specs=[pl.BlockSpec((rows, group * LANES), lambda g, i: (first + i, g))]
        + _kv_specs(n_lat + n_ctx, n_lat // ATTN_TK, LANES, 0),
        out_specs=pl.BlockSpec((rows, group * LANES), lambda g, i: (i, g)),
        out_shape=jax.ShapeDtypeStruct((steps * rows, QA_COLS), BF16),
        scratch_shapes=_attn_scratch(row_blocks * group),
        compiler_params=_params("arbitrary", "arbitrary"),
        name="gqa_attention" if latent else "gqa_attention_ctx",
    )(q, k, vt)


DIFF_HEADS_PER_STEP = 2


def _diff_kernel(q_ref, k_ref, vt_ref, lam_ref, gs_ref, o_ref, *scratch,
                 n_lat, n_ctx, lambda_init, latent, row_blocks):
    heads = DIFF_HEADS_PER_STEP
    tiles = _make_tiles(scratch, [h for _ in range(row_blocks) for h in range(heads) for _ in range(2)])
    lane = lax.broadcasted_iota(jnp.int32, (ATTN_TQ, LANES), 1)
    for rb in range(row_blocks):
        for h in range(heads):
            qh = q_ref[_row_block(rb), h * LANES:(h + 1) * LANES].astype(F32)
            zero = jnp.zeros_like(qh)
            first = (rb * heads + h) * 2
            tiles[first].qt[...] = jnp.where(lane < HEAD_DIM_B, qh, zero).T.astype(BF16)
            tiles[first + 1].qt[...] = jnp.where(lane >= HEAD_DIM_B, qh, zero).T.astype(BF16)
    _flash_t(tiles, k_ref, vt_ref, n_lat=n_lat, n_ctx=n_ctx, latent=latent)
    lv = lam_ref[...]
    lam = (jnp.exp(jnp.sum(lv[0:1] * lv[1:2], axis=-1, keepdims=True))
           - jnp.exp(jnp.sum(lv[2:3] * lv[3:4], axis=-1, keepdims=True)) + lambda_init)
    for rb in range(row_blocks):
        for h in range(heads):
            first = (rb * heads + h) * 2
            out1, out2 = (t.acc[...] / t.l[...] for t in tiles[first:first + 2])
            diff = (out1 - lam * out2).T
            o_ref[_row_block(rb), h * LANES:(h + 1) * LANES] = (
                _head_rmsnorm(diff, gs_ref[...]) * (1.0 - lambda_init)).astype(o_ref.dtype)


def _diff_attention(q, k, vt, lam_vecs, g_subln, lambda_init, n_lat, n_ctx, latent):
    width = DIFF_HEADS_PER_STEP * LANES
    assert QA_COLS % width == 0 and KA_COLS % width == 0
    row_blocks, steps, first = _attn_rows(n_lat, n_ctx, latent)
    rows = row_blocks * ATTN_TQ
    return pl.pallas_call(
        functools.partial(_diff_kernel, n_lat=n_lat, n_ctx=n_ctx, lambda_init=lambda_init,
                          latent=latent, row_blocks=row_blocks),
        grid=(N_HEADS_B // DIFF_HEADS_PER_STEP, steps),
        in_specs=[pl.BlockSpec((rows, width), lambda h, i: (first + i, QA_COLS // width + h))]
        + _kv_specs(n_lat + n_ctx, n_lat // ATTN_TK, width, KA_COLS // width)
        + [pl.BlockSpec((4, HEAD_DIM_B), lambda h, i: (0, 0)),
           pl.BlockSpec((1, LANES), lambda h, i: (0, 0))],
        out_specs=pl.BlockSpec((rows, width), lambda h, i: (i, h)),
        out_shape=jax.ShapeDtypeStruct((steps * rows, VB_COLS), BF16),
        scratch_shapes=_attn_scratch(row_blocks * DIFF_HEADS_PER_STEP * 2),
        compiler_params=_params("arbitrary", "arbitrary"),
        name="diff_attention" if latent else "diff_attention_ctx",
    )(q, k, vt, lam_vecs, g_subln.reshape(1, LANES))


def _dft_cos_sin(n):
    j = np.arange(n, dtype=np.float64)
    ang = 2.0 * np.pi * np.outer(j, j) / n
    return np.cos(ang), np.sin(ang)


def _channel_mix_weights(cs_ref, wf_ref, ab_ref):
    cs = cs_ref[...].astype(BF16)
    for g in range(N_FOURIER_GROUPS):
        ab_ref[g] = jnp.dot(cs, wf_ref[g].astype(BF16), preferred_element_type=F32).astype(BF16)


def _channel_mix(xr, xi, ab_ref, norm):
    outs = []
    for g in range(N_FOURIER_GROUPS):
        sl = slice(g * LANES, (g + 1) * LANES)
        lhs = jnp.concatenate([xr[:, sl], xi[:, sl]], axis=-1).astype(BF16)
        outs.append(jnp.dot(lhs, ab_ref[g], preferred_element_type=F32))
    return jnp.concatenate(outs, axis=-1) * norm


def _fourier_stage1_kernel(w_ref, u_ref, o_ref):
    o_ref[...] = jnp.dot(w_ref[...].astype(BF16), u_ref[...],
                         preferred_element_type=F32).astype(o_ref.dtype)


def _fourier_stage2_kernel(a_ref, tc_ref, ts_ref, w3_ref, cs_ref, wf_ref, o_ref, ab_ref, *, norm):
    @pl.when(pl.program_id(0) == 0)
    def _():
        _channel_mix_weights(cs_ref, wf_ref, ab_ref)

    ar = a_ref[0, 0].astype(F32)
    ai = a_ref[1, 0].astype(F32)
    tc, ts = tc_ref[0], ts_ref[0]
    br = ar * tc + ai * ts
    bi = ai * tc - ar * ts
    stack = jnp.concatenate([br, bi], axis=0).astype(BF16)
    x = jnp.dot(w3_ref[...].astype(BF16), stack, preferred_element_type=F32)
    half = x.shape[0] // 2
    o_ref[...] = _channel_mix(x[:half], x[half:], ab_ref, norm).astype(o_ref.dtype)


def _fourier_ctx_kernel(u_ref, w_ref, cs_ref, wf_ref, o_ref, ab_ref, *, norm):
    _channel_mix_weights(cs_ref, wf_ref, ab_ref)
    x = jnp.dot(w_ref[...].astype(BF16), u_ref[...], preferred_element_type=F32)
    half = x.shape[0] // 2
    o_ref[...] = _channel_mix(x[:half], x[half:], ab_ref, norm).astype(o_ref.dtype)


def _fourier_latent(u, n, w_fourier):
    assert u.shape[0] % LANES == 0
    n2 = LANES
    n1 = n // n2
    c1, s1 = _dft_cos_sin(n1)
    w1 = jnp.asarray(np.concatenate([c1, -s1], axis=0), F32)
    k1n2 = 2.0 * np.pi * np.outer(np.arange(n1), np.arange(n2)) / n
    tw_c = jnp.asarray(np.cos(k1n2)[:, :, None], F32)
    tw_s = jnp.asarray(np.sin(k1n2)[:, :, None], F32)
    c2, s2 = _dft_cos_sin(n2)
    w3 = jnp.asarray(np.block([[c2, s2], [-s2, c2]]), F32)
    cc, sc = _dft_cos_sin(FOURIER_GROUP_DIM)
    cs = jnp.asarray(np.concatenate([cc, sc], axis=0), F32)
    width = n2 * C_COLS
    cb = _largest_tile(width, (4096, 2048, 1024, 512))
    stage1 = pl.pallas_call(
        _fourier_stage1_kernel,
        grid=(width // cb,),
        in_specs=[pl.BlockSpec((2 * n1, n1), lambda j: (0, 0)),
                  pl.BlockSpec((n1, cb), lambda j: (0, j))],
        out_specs=pl.BlockSpec((2 * n1, cb), lambda j: (0, j)),
        out_shape=jax.ShapeDtypeStruct((2 * n1, width), BF16),
        compiler_params=_params("arbitrary"),
        name="fourier_stage1",
    )(w1, u.reshape(u.shape[0] // n2, width))
    a = stage1.reshape(2, n1, n2, C_COLS)
    norm = 1.0 / math.sqrt(n * FOURIER_GROUP_DIM)
    out = pl.pallas_call(
        functools.partial(_fourier_stage2_kernel, norm=norm),
        grid=(n1,),
        in_specs=[pl.BlockSpec((2, 1, n2, C_COLS), lambda k: (0, k, 0, 0)),
                  pl.BlockSpec((1, n2, 1), lambda k: (k, 0, 0)),
                  pl.BlockSpec((1, n2, 1), lambda k: (k, 0, 0)),
                  pl.BlockSpec((2 * n2, 2 * n2), lambda k: (0, 0)),
                  pl.BlockSpec((2 * FOURIER_GROUP_DIM, FOURIER_GROUP_DIM), lambda k: (0, 0)),
                  pl.BlockSpec((N_FOURIER_GROUPS, FOURIER_GROUP_DIM, FOURIER_GROUP_DIM),
                               lambda k: (0, 0, 0))],
        out_specs=pl.BlockSpec((n2, C_COLS), lambda k: (0, k)),
        out_shape=jax.ShapeDtypeStruct((n2, n1 * C_COLS), BF16),
        scratch_shapes=[pltpu.VMEM((N_FOURIER_GROUPS, 2 * FOURIER_GROUP_DIM, FOURIER_GROUP_DIM), BF16)],
        compiler_params=_params("arbitrary"),
        name="fourier_stage2",
    )(a, tw_c, tw_s, w3, cs, w_fourier)
    return out.reshape(n, C_COLS)


def _fourier_ctx(u_ctx, w_fourier):
    m = u_ctx.shape[0]
    c, s = _dft_cos_sin(m)
    w = jnp.asarray(np.concatenate([c, -s], axis=0), F32)
    cc, sc = _dft_cos_sin(FOURIER_GROUP_DIM)
    cs = jnp.asarray(np.concatenate([cc, sc], axis=0), F32)
    norm = 1.0 / math.sqrt(m * FOURIER_GROUP_DIM)
    return pl.pallas_call(
        functools.partial(_fourier_ctx_kernel, norm=norm),
        out_shape=jax.ShapeDtypeStruct((m, C_COLS), BF16),
        scratch_shapes=[pltpu.VMEM((N_FOURIER_GROUPS, 2 * FOURIER_GROUP_DIM, FOURIER_GROUP_DIM), BF16)],
        compiler_params=pltpu.CompilerParams(vmem_limit_bytes=VMEM_LIMIT),
        name="fourier_ctx",
    )(u_ctx, w, cs, w_fourier)


def _gated_residual(x, f, g, mod, gate_row):
    ms = jnp.mean(f * f, axis=-1, keepdims=True)
    return x + mod[gate_row:gate_row + 1] * (f * lax.rsqrt(ms + NORM_EPS) * g)


def _residual_kernel(x_ref, f_ref, g_ref, mod_ref, o_ref, *, gate_row):
    o_ref[...] = _gated_residual(x_ref[...], f_ref[...], g_ref[...], mod_ref[0], gate_row)


def _post_residual(x, f, g, mod, gate_row, n_rows, n_lat):
    d = x.shape[1]
    row = pl.BlockSpec((ROW_TILE, d), lambda i: (i, 0))
    return pl.pallas_call(
        functools.partial(_residual_kernel, gate_row=gate_row),
        grid=(n_rows // ROW_TILE,),
        in_specs=[row, row, pl.BlockSpec((1, d), lambda i: (0, 0)), _mod_spec(d, n_lat // ROW_TILE)],
        out_specs=row,
        out_shape=jax.ShapeDtypeStruct((n_rows, d), F32),
        compiler_params=_params("arbitrary"),
        name="post_residual",
    )(x, f, g.reshape(1, d), mod)


def _outproj_residual_kernel(*refs, n_pieces, gate_row):
    y_refs = refs[:n_pieces]
    w_ref, x_ref, g_ref, mod_ref, o_ref = refs[n_pieces:]
    y = jnp.concatenate([r[...] for r in y_refs], axis=-1)
    f = jnp.dot(y, w_ref[...], preferred_element_type=F32)
    o_ref[...] = _gated_residual(x_ref[...], f, g_ref[...], mod_ref[0], gate_row)


def _outproj_residual(y_pieces, w_bf16, x, g, mod, gate_row, n_rows, n_lat):
    d = x.shape[1]
    k = w_bf16.shape[0]
    assert sum(piece.shape[1] for piece in y_pieces) == k
    row = lambda w: pl.BlockSpec((ROW_TILE, w), lambda i: (i, 0))
    return pl.pallas_call(
        functools.partial(_outproj_residual_kernel, n_pieces=len(y_pieces), gate_row=gate_row),
        grid=(n_rows // ROW_TILE,),
        in_specs=[row(piece.shape[1]) for piece in y_pieces]
        + [pl.BlockSpec((k, d), lambda i: (0, 0)), row(d), pl.BlockSpec((1, d), lambda i: (0, 0)),
           _mod_spec(d, n_lat // ROW_TILE)],
        out_specs=row(d),
        out_shape=jax.ShapeDtypeStruct((n_rows, d), F32),
        compiler_params=_params("arbitrary"),
        name="out_proj_residual",
    )(*y_pieces, w_bf16, x, g.reshape(1, d), mod)


MOE_TM = 256
DISPATCH_TM = 2 * MOE_TM
R_IDX, R_W, R_RANK = 0, 2, 4


def _router_kernel(x_ref, g_ref, mod_ref, wr_ref, h_ref, route_ref, cnt_ref, carry_ref, *, shift_row):
    @pl.when(pl.program_id(0) == 0)
    def _():
        carry_ref[...] = jnp.zeros(carry_ref.shape, F32)

    h = _norm_modulate(x_ref[...], g_ref[...], mod_ref[0], shift_row)
    h_ref[...] = h
    logits = jnp.dot(h, wr_ref[...], preferred_element_type=F32, precision=lax.Precision.HIGHEST)
    rows = logits.shape[0]
    lane = lax.broadcasted_iota(jnp.int32, logits.shape, 1).astype(F32)
    neg = jnp.full(logits.shape, -jnp.inf, F32)
    l1 = jnp.where(lane < N_EXPERTS, logits, neg)
    m1 = jnp.max(l1, axis=-1, keepdims=True)
    i1 = jnp.min(jnp.where(l1 == m1, lane, float(LANES)), axis=-1, keepdims=True)
    l2 = jnp.where(lane == i1, neg, l1)
    m2 = jnp.max(l2, axis=-1, keepdims=True)
    i2 = jnp.min(jnp.where(l2 == m2, lane, float(LANES)), axis=-1, keepdims=True)
    e2 = jnp.exp(m2 - m1)
    w1 = 1.0 / (1.0 + e2)
    w2 = e2 / (1.0 + e2)
    hit1 = lane == i1
    hit2 = lane == i2
    onehot = jnp.logical_or(hit1, hit2).astype(BF16)
    r_i = lax.broadcasted_iota(jnp.int32, (rows, rows), 0)
    c_i = lax.broadcasted_iota(jnp.int32, (rows, rows), 1)
    before = (c_i < r_i).astype(BF16)
    base = carry_ref[...] + jnp.dot(before, onehot, preferred_element_type=F32)
    rank1 = jnp.sum(jnp.where(hit1, base, 0.0), axis=-1, keepdims=True)
    rank2 = jnp.sum(jnp.where(hit2, base, 0.0), axis=-1, keepdims=True)
    carry_ref[...] = carry_ref[...] + jnp.sum(onehot.astype(F32), axis=0, keepdims=True)
    rec = jnp.zeros(logits.shape, F32)
    for off, val in ((R_IDX, i1), (R_IDX + 1, i2), (R_W, w1), (R_W + 1, w2),
                     (R_RANK, rank1), (R_RANK + 1, rank2)):
        rec = jnp.where(lane == off, val, rec)
    route_ref[...] = rec
    cnt_ref[...] = jnp.broadcast_to(carry_ref[...], cnt_ref.shape)


def _router(x, g, mod, w_router, n_lat):
    d = x.shape[1]
    wr = jnp.zeros((d, LANES), F32).at[:, :N_EXPERTS].set(w_router)
    return pl.pallas_call(
        functools.partial(_router_kernel, shift_row=3),
        grid=(n_lat // ROW_TILE,),
        in_specs=[pl.BlockSpec((ROW_TILE, d), lambda i: (i, 0)),
                  pl.BlockSpec((1, d), lambda i: (0, 0)),
                  pl.BlockSpec((1, N_MOD, d), lambda i: (0, 0, 0)),
                  pl.BlockSpec((d, LANES), lambda i: (0, 0))],
        out_specs=[pl.BlockSpec((ROW_TILE, d), lambda i: (i, 0)),
                   pl.BlockSpec((ROW_TILE, LANES), lambda i: (i, 0)),
                   pl.BlockSpec((8, LANES), lambda i: (0, 0))],
        out_shape=[jax.ShapeDtypeStruct((n_lat, d), F32),
                   jax.ShapeDtypeStruct((n_lat, LANES), F32),
                   jax.ShapeDtypeStruct((8, LANES), F32)],
        scratch_shapes=[pltpu.VMEM((1, LANES), F32)],
        compiler_params=_params("arbitrary"),
        name="router",
    )(x, g.reshape(1, d), mod, wr)


def _row_copy(src_hbm, src_row, dst_ref, dst_row, sem):
    return pltpu.make_async_copy(src_hbm.at[pl.ds(src_row, 1)], dst_ref.at[pl.ds(dst_row, 1)], sem)


def _wait_rows(src_hbm, dst_ref, sem):
    pltpu.make_async_copy(src_hbm.at[pl.ds(0, dst_ref.shape[0])], dst_ref, sem).wait()


ISSUE_UNROLL = 8


def _dispatch_kernel(src_ref, nv_ref, h_hbm, o_ref, buf_ref, sems):
    i = pl.program_id(0)
    n_valid = nv_ref[0]
    slot = lax.rem(i, 2)

    def issue(tile, slot):
        base = tile * DISPATCH_TM

        def body(r, carry):
            _row_copy(h_hbm, src_ref[base + r], buf_ref.at[slot], r, sems.at[slot]).start()
            return carry
        lax.fori_loop(0, DISPATCH_TM, body, 0, unroll=ISSUE_UNROLL)

    @pl.when(jnp.logical_and(i == 0, n_valid > 0))
    def _():
        issue(0, 0)

    @pl.when(i + 1 < n_valid)
    def _():
        issue(i + 1, 1 - slot)

    @pl.when(i < n_valid)
    def _():
        _wait_rows(h_hbm, buf_ref.at[slot], sems.at[slot])
        o_ref[...] = buf_ref[slot].astype(o_ref.dtype)

    @pl.when(i >= n_valid)
    def _():
        o_ref[...] = jnp.zeros(o_ref.shape, o_ref.dtype)


def _dispatch(src_token, n_valid, h):
    d = h.shape[1]
    n_slots = src_token.shape[0]
    grid_spec = pltpu.PrefetchScalarGridSpec(
        num_scalar_prefetch=2,
        grid=(n_slots // DISPATCH_TM,),
        in_specs=[pl.BlockSpec(memory_space=pl.ANY)],
        out_specs=pl.BlockSpec((DISPATCH_TM, d), lambda i, src, nv: (i, 0)),
        scratch_shapes=[pltpu.VMEM((2, DISPATCH_TM, d), F32), pltpu.SemaphoreType.DMA((2,))],
    )
    return pl.pallas_call(
        _dispatch_kernel,
        grid_spec=grid_spec,
        out_shape=jax.ShapeDtypeStruct((n_slots, d), BF16),
        compiler_params=_params("arbitrary"),
        name="moe_dispatch",
    )(src_token, n_valid, h)


def _combine_kernel(dest_ref, y_hbm, route_ref, x_ref, g_ref, mod_ref, o_ref, buf_ref, sems, *, gate_row):
    i = pl.program_id(0)
    slot = lax.rem(i, 2)

    def issue(tile, slot):
        base = tile * ROW_TILE

        def body(r, carry):
            for k in range(TOP_K):
                _row_copy(y_hbm, dest_ref[TOP_K * (base + r) + k], buf_ref.at[slot], k * ROW_TILE + r,
                          sems.at[slot]).start()
            return carry
        lax.fori_loop(0, ROW_TILE, body, 0, unroll=ISSUE_UNROLL)

    @pl.when(i == 0)
    def _():
        issue(0, 0)

    @pl.when(i + 1 < pl.num_programs(0))
    def _():
        issue(i + 1, 1 - slot)

    _wait_rows(y_hbm, buf_ref.at[slot], sems.at[slot])
    route = route_ref[...]
    f = (route[:, R_W:R_W + 1] * buf_ref[slot, 0:ROW_TILE, :]
         + route[:, R_W + 1:R_W + 2] * buf_ref[slot, ROW_TILE:TOP_K * ROW_TILE, :])
    o_ref[...] = _gated_residual(x_ref[...], f, g_ref[...], mod_ref[0], gate_row)


def _combine_residual(dest_flat, y_sorted, route, x, g, mod, gate_row, n_lat):
    d = x.shape[1]
    row = lambda w: pl.BlockSpec((ROW_TILE, w), lambda i, dest: (i, 0))
    grid_spec = pltpu.PrefetchScalarGridSpec(
        num_scalar_prefetch=1,
        grid=(n_lat // ROW_TILE,),
        in_specs=[pl.BlockSpec(memory_space=pl.ANY), row(LANES), row(d),
                  pl.BlockSpec((1, d), lambda i, dest: (0, 0)),
                  pl.BlockSpec((1, N_MOD, d), lambda i, dest: (0, 0, 0))],
        out_specs=row(d),
        scratch_shapes=[pltpu.VMEM((2, TOP_K * ROW_TILE, d), F32), pltpu.SemaphoreType.DMA((2,))],
    )
    return pl.pallas_call(
        functools.partial(_combine_kernel, gate_row=gate_row),
        grid_spec=grid_spec,
        out_shape=jax.ShapeDtypeStruct((n_lat, d), F32),
        compiler_params=_params("arbitrary"),
        name="moe_combine",
    )(dest_flat, y_sorted, route, x, g.reshape(1, d), mod)


def _moe_layer(x, g_pre, g_post, mod, w_router, w_gate, w_up, w_down, n_lat):
    d_ff = w_gate.shape[-1]
    h, route, cnt = _router(x, g_pre, mod, w_router, n_lat)
    experts = route[:, R_IDX:R_IDX + TOP_K].astype(jnp.int32)
    ranks = route[:, R_RANK:R_RANK + TOP_K].astype(jnp.int32)
    counts = cnt[0, :N_EXPERTS].astype(jnp.int32)
    padded = ((counts + MOE_TM - 1) // MOE_TM) * MOE_TM
    ends = jnp.cumsum(padded)
    starts = ends - padded
    dest = (starts[experts] + ranks).reshape(-1)
    n_slots = TOP_K * n_lat + N_EXPERTS * MOE_TM
    assert n_slots % DISPATCH_TM == 0
    n_valid = ((ends[-1:] + DISPATCH_TM - 1) // DISPATCH_TM).astype(jnp.int32)

    token = jnp.repeat(jnp.arange(n_lat, dtype=jnp.int32), TOP_K)
    src_token = jnp.zeros((n_slots,), jnp.int32).at[dest].set(token)
    hs = _dispatch(src_token, n_valid, h)
    seg_start = jnp.concatenate([starts, ends[-1:]]).astype(jnp.int32)
    seg_tiles = (jnp.concatenate([padded, n_slots - ends[-1:]]) // MOE_TM).astype(jnp.int32)
    act = _expert_matmul(hs, [w_gate, w_up], seg_start, seg_tiles, tm=MOE_TM,
                         tn=_largest_tile(d_ff, (1024, 512, 256, 128)), out_dtype=BF16, swiglu=True,
                         name="moe_gate_up")
    y = _expert_matmul(act, [w_down], seg_start, seg_tiles, tm=MOE_TM,
                       tn=_largest_tile(x.shape[1], (512, 256, 128)), out_dtype=F32, name="moe_down")
    return _combine_residual(dest, y, route, x, g_post, mod, 5, n_lat)


def kernel(x, c, ctx, c_ctx, w_mod, b_mod, g_pre_mix, g_post_mix, g_pre_ffn, g_post_ffn, w_in, g_q_a, g_k_a, lambda_q1, lambda_k1, lambda_q2, lambda_k2, g_subln_b, w_fourier, w_out, w_gate_dense, w_up_dense, w_down_dense, w_router, w_gate_moe, w_up_moe, w_down_moe):
    batch, n_lat, d = x.shape
    n_ctx = ctx.shape[1]
    depth = w_mod.shape[0]
    assert batch == 1 and n_ctx == ROW_TILE and n_lat % max(2 * ATTN_TK, MOE_TM, GRID_W) == 0
    assert depth == 2, "layer 0 dense, layer 1 (last) routed experts"
    t = n_lat + n_ctx

    cond_rows = jnp.concatenate([c, c_ctx[None, :], jnp.zeros((6, d), F32)], axis=0)
    mod_all = _modulation(cond_rows, w_mod, b_mod)
    rope_a = _rope_tables(n_lat, n_ctx, HEAD_DIM_A, 1)
    rope_b = _rope_tables(n_lat, n_ctx, HEAD_DIM_B, 2)

    xs = jnp.concatenate([x[0], ctx[0]], axis=0)
    for layer in range(depth):
        last = layer == depth - 1
        n_out = n_lat if last else t
        mod = mod_all[layer, :2].reshape(2, N_MOD, d)
        lambda_init = 0.8 - 0.6 * math.exp(-0.3 * layer)

        q, k, u, vt = _mixer_front(xs, g_pre_mix[layer], mod, w_in[layer].astype(BF16),
                                   g_q_a[layer], g_k_a[layer], rope_a, rope_b, n_lat)
        lam_vecs = jnp.stack([lambda_q1[layer], lambda_k1[layer], lambda_q2[layer], lambda_k2[layer]])
        mixers = [[_gqa_attention(q, k, vt, n_lat, n_ctx, latent),
                   _diff_attention(q, k, vt, lam_vecs, g_subln_b[layer], lambda_init, n_lat, n_ctx, latent),
                   _fourier_latent(u, n_lat, w_fourier[layer]) if latent
                   else _fourier_ctx(u[n_lat:], w_fourier[layer])]
                  for latent in ((True,) if last else (True, False))]
        y_a, y_b, y_c = (jnp.concatenate(parts, axis=0) for parts in zip(*mixers))
        xs = _outproj_residual([y_a, y_b, y_c], w_out[layer].astype(BF16), xs, g_post_mix[layer], mod, 2,
                               n_out, n_lat)

        j = layer // 2
        if layer % 2 == 0:
            h = _prenorm(xs, g_pre_ffn[layer], mod, 3, n_out, n_lat)
            act = _dense_matmul(h, w_gate_dense[j], w2=w_up_dense[j], swiglu=True, tn=512,
                                out_dtype=BF16, name="ffn_gate_up")
            f = _dense_matmul(act, w_down_dense[j], tn=512, tiles=(ROW_TILE,), out_dtype=F32, name="ffn_down")
            xs = _post_residual(xs, f, g_post_ffn[layer], mod, 5, n_out, n_lat)
        else:
            xs = _moe_layer(xs, g_pre_ffn[layer], g_post_ffn[layer], mod, w_router[j],
                            w_gate_moe[j], w_up_moe[j], w_down_moe[j], n_lat)
    return xs[:n_lat][None]
```

```python
import functools
import math

import numpy as np
import jax
import jax.numpy as jnp
from jax import lax
from jax.experimental import pallas as pl
from jax.experimental.pallas import tpu as pltpu

F32 = jnp.float32
BF16 = jnp.bfloat16

GRID_W = 64
ROPE_THETA = 10000.0
NORM_EPS = 1e-6
N_HEADS_A, N_KV_HEADS_A, HEAD_DIM_A = 8, 2, 128
N_HEADS_B, HEAD_DIM_B = 4, 64
VALUE_DIM_B = 2 * HEAD_DIM_B
N_FOURIER_GROUPS, FOURIER_GROUP_DIM = 4, 128
QA_COLS = N_HEADS_A * HEAD_DIM_A
QB_COLS = N_HEADS_B * 2 * HEAD_DIM_B
C_COLS = N_FOURIER_GROUPS * FOURIER_GROUP_DIM
KA_COLS = N_KV_HEADS_A * HEAD_DIM_A
VA_COLS = KA_COLS
KB_COLS = QB_COLS
VB_COLS = N_HEADS_B * VALUE_DIM_B
KV_START = QA_COLS + QB_COLS + C_COLS
IN_COLS = KV_START + KA_COLS + VA_COLS + KB_COLS + VB_COLS
N_EXPERTS, TOP_K = 8, 2
N_MOD = 6

LANES = 128
V7X_VMEM_BYTES = 64 * 1024 * 1024
VMEM_LIMIT = 52 * 1024 * 1024

ROW_TILE = 256
NEG_BIG = -1e30
LOG2_E = math.log2(math.e)


def _params(*sem):
    return pltpu.CompilerParams(dimension_semantics=sem, vmem_limit_bytes=VMEM_LIMIT)


def _silu(x):
    return x / (1.0 + jnp.exp(-x))


def _largest_tile(n, candidates):
    for c in candidates:
        if n % c == 0:
            return c
    raise ValueError(f"no tile in {candidates} divides {n}")


def _mod_kernel(c_ref, w_ref, b_ref, o_ref):
    cond = _silu(c_ref[...])
    o_ref[0] = jnp.dot(cond.astype(BF16), w_ref[0].astype(BF16),
                       preferred_element_type=F32) + b_ref[0]


def _modulation(cond_rows, w_mod, b_mod):
    depth, d, n6 = w_mod.shape
    tn = _largest_tile(n6, (1024, 512, 256, 128))
    return pl.pallas_call(
        _mod_kernel,
        grid=(depth, n6 // tn),
        in_specs=[pl.BlockSpec((8, d), lambda l, j: (0, 0)),
                  pl.BlockSpec((1, d, tn), lambda l, j: (l, 0, j)),
                  pl.BlockSpec((1, 1, tn), lambda l, j: (l, 0, j))],
        out_specs=pl.BlockSpec((1, 8, tn), lambda l, j: (l, 0, j)),
        out_shape=jax.ShapeDtypeStruct((depth, 8, n6), F32),
        compiler_params=_params("arbitrary", "arbitrary"),
        name="modulation",
    )(cond_rows, w_mod, b_mod.reshape(depth, 1, n6))


def _norm_modulate(x, g, mod, shift_row):
    ms = jnp.mean(x * x, axis=-1, keepdims=True)
    y = x * lax.rsqrt(ms + NORM_EPS) * g
    return y * (1.0 + mod[shift_row + 1:shift_row + 2]) + mod[shift_row:shift_row + 1]


def _prenorm_kernel(x_ref, g_ref, mod_ref, o_ref, *, shift_row):
    h = _norm_modulate(x_ref[...], g_ref[...], mod_ref[0], shift_row)
    o_ref[...] = h.astype(o_ref.dtype)


def _mod_spec(d, n_lat_tiles):
    return pl.BlockSpec((1, N_MOD, d), lambda i: ((i >= n_lat_tiles).astype(jnp.int32), 0, 0))


def _prenorm(x, g, mod, shift_row, n_rows, n_lat):
    d = x.shape[1]
    return pl.pallas_call(
        functools.partial(_prenorm_kernel, shift_row=shift_row),
        grid=(n_rows // ROW_TILE,),
        in_specs=[pl.BlockSpec((ROW_TILE, d), lambda i: (i, 0)),
                  pl.BlockSpec((1, d), lambda i: (0, 0)),
                  _mod_spec(d, n_lat // ROW_TILE)],
        out_specs=pl.BlockSpec((ROW_TILE, d), lambda i: (i, 0)),
        out_shape=jax.ShapeDtypeStruct((n_rows, d), BF16),
        compiler_params=_params("arbitrary"),
        name="prenorm",
    )(x, g.reshape(1, d), mod)


def _matmul_act(x, wb_ref, swiglu):
    if swiglu:
        gate = jnp.dot(x, wb_ref[0], preferred_element_type=F32)
        up = jnp.dot(x, wb_ref[1], preferred_element_type=F32)
        return _silu(gate) * up
    return jnp.dot(x, wb_ref[0], preferred_element_type=F32)


def _dense_mm_kernel(lhs_ref, *refs, n_w, swiglu):
    w_refs, (o_ref, wb_ref) = refs[:n_w], refs[n_w:]

    @pl.when(pl.program_id(1) == 0)
    def _():
        for i in range(n_w):
            wb_ref[i] = w_refs[i][...].astype(BF16)

    o_ref[...] = _matmul_act(lhs_ref[...], wb_ref, swiglu).astype(o_ref.dtype)


def _expert_mm_kernel(start_ref, tiles_ref, lhs_hbm, *refs, n_w, n_experts, swiglu, tm, group, tn):
    w_refs = refs[:n_w]
    out_hbm, wb_ref, lhs_buf, out_buf, in_sems, out_sems = refs[n_w:]
    e = pl.program_id(1)
    row0 = start_ref[e]
    cnt = tiles_ref[e]
    col0 = pl.multiple_of(pl.program_id(0) * tn, tn)

    def out_copy(rows, first_row, i, slot):
        r = pl.multiple_of(first_row + i * rows, tm)
        return pltpu.make_async_copy(out_buf.at[slot, pl.ds(0, rows)],
                                     out_hbm.at[pl.ds(r, rows), pl.ds(col0, tn)], out_sems.at[slot])

    def stream(rows, first_row, n_iter):
        def in_copy(i, slot):
            r = pl.multiple_of(first_row + i * rows, tm)
            return pltpu.make_async_copy(lhs_hbm.at[pl.ds(r, rows)], lhs_buf.at[slot, pl.ds(0, rows)],
                                         in_sems.at[slot])

        @pl.when(n_iter > 0)
        def _():
            in_copy(0, 0).start()

        def body(i, carry):
            slot = lax.rem(i, 2)

            @pl.when(i + 1 < n_iter)
            def _():
                in_copy(i + 1, 1 - slot).start()

            in_copy(i, slot).wait()

            @pl.when(i >= 2)
            def _():
                out_copy(rows, first_row, i - 2, slot).wait()

            out_buf[slot, 0:rows, :] = _matmul_act(lhs_buf[slot, 0:rows, :], wb_ref, swiglu).astype(
                out_buf.dtype)
            out_copy(rows, first_row, i, slot).start()
            return carry
        lax.fori_loop(0, n_iter, body, 0)

        @pl.when(n_iter >= 2)
        def _():
            out_copy(rows, first_row, n_iter - 2, lax.rem(n_iter, 2)).wait()

        @pl.when(n_iter >= 1)
        def _():
            out_copy(rows, first_row, n_iter - 1, lax.rem(n_iter + 1, 2)).wait()

    @pl.when(e < n_experts)
    def _():
        for i in range(n_w):
            wb_ref[i] = w_refs[i][0].astype(BF16)
        n_big = cnt // group if group > 1 else 0
        if group > 1:
            stream(group * tm, row0, n_big)
        stream(tm, row0 + n_big * (group * tm), cnt - n_big * group)

    @pl.when(e == n_experts)
    def _():
        out_buf[0, 0:tm, :] = jnp.zeros((tm, tn), out_buf.dtype)

        def body(i, carry):
            out_copy(tm, row0, i, 0).start()
            out_copy(tm, row0, i, 0).wait()
            return carry
        lax.fori_loop(0, cnt, body, 0)


def _expert_matmul(lhs, weights, seg_start, seg_tiles, *, tm, group, tn, out_dtype, swiglu=False,
                   name="expert_mm"):
    n_experts, k, nw = weights[0].shape
    n_w = len(weights)
    grid_spec = pltpu.PrefetchScalarGridSpec(
        num_scalar_prefetch=2,
        grid=(nw // tn, n_experts + 1),
        in_specs=[pl.BlockSpec(memory_space=pl.ANY)]
        + [pl.BlockSpec((1, k, tn), lambda n, e, st, ti: (jnp.minimum(e, n_experts - 1), 0, n))
           for _ in range(n_w)],
        out_specs=pl.BlockSpec(memory_space=pl.ANY),
        scratch_shapes=[pltpu.VMEM((n_w, k, tn), BF16), pltpu.VMEM((2, group * tm, k), BF16),
                        pltpu.VMEM((2, group * tm, tn), out_dtype),
                        pltpu.SemaphoreType.DMA((2,)), pltpu.SemaphoreType.DMA((2,))],
    )
    return pl.pallas_call(
        functools.partial(_expert_mm_kernel, n_w=n_w, n_experts=n_experts, swiglu=swiglu, tm=tm,
                          group=group, tn=tn),
        grid_spec=grid_spec,
        out_shape=jax.ShapeDtypeStruct((lhs.shape[0], nw), out_dtype),
        compiler_params=_params("arbitrary", "arbitrary"),
        name=name,
    )(seg_start, seg_tiles, lhs, *weights)


def _dense_matmul(lhs, w, *, tn, out_dtype, tiles=(1024, 768, 512, 256), swiglu=False, w2=None,
                  name="matmul"):
    n_rows, k = lhs.shape
    nw = w.shape[1]
    tm = _largest_tile(n_rows, tiles)
    weights = [w] if w2 is None else [w, w2]
    return pl.pallas_call(
        functools.partial(_dense_mm_kernel, n_w=len(weights), swiglu=swiglu),
        grid=(nw // tn, n_rows // tm),
        in_specs=[pl.BlockSpec((tm, k), lambda n, m: (m, 0))]
        + [pl.BlockSpec((k, tn), lambda n, m: (0, n)) for _ in weights],
        out_specs=pl.BlockSpec((tm, tn), lambda n, m: (m, n)),
        out_shape=jax.ShapeDtypeStruct((n_rows, nw), out_dtype),
        scratch_shapes=[pltpu.VMEM((len(weights), k, tn), BF16)],
        compiler_params=_params("arbitrary", "arbitrary"),
        name=name,
    )(lhs, *weights)


def _rope_tables(n_lat, n_ctx, head_dim, reps):
    n_rows_grid = n_lat // GRID_W
    rows = jnp.repeat(jnp.arange(n_rows_grid), GRID_W).astype(F32)
    cols = jnp.tile(jnp.arange(GRID_W), n_rows_grid).astype(F32)
    axis_dim = head_dim // 2
    freqs = ROPE_THETA ** (-jnp.arange(0, axis_dim, 2, dtype=F32) / axis_dim)
    ra, ca = rows[:, None] * freqs, cols[:, None] * freqs
    rc, rs, cc, cs = jnp.cos(ra), jnp.sin(ra), jnp.cos(ca), jnp.sin(ca)
    z = jnp.zeros_like(rs)
    cos = jnp.concatenate([rc, rc, cc, cc], axis=-1)
    sin_up = jnp.concatenate([-rs, z, -cs, z], axis=-1)
    sin_dn = jnp.concatenate([z, rs, z, cs], axis=-1)
    pad = lambda t, v: jnp.concatenate(
        [jnp.tile(t, (1, reps)), jnp.full((n_ctx, LANES), v, F32)], axis=0)
    return pad(cos, 1.0), pad(sin_up, 0.0), pad(sin_dn, 0.0)


def _rope(x, cos, sin_up, sin_dn, quarter):
    return (x * cos + pltpu.roll(x, LANES - quarter, 1) * sin_up
            + pltpu.roll(x, quarter, 1) * sin_dn)


def _head_rmsnorm(x, g):
    return x * lax.rsqrt(jnp.mean(x * x, axis=-1, keepdims=True) + NORM_EPS) * g


def _qkpost_kernel(p_ref, gq_ref, gk_ref, ca_ref, ua_ref, da_ref, cb_ref, ub_ref, db_ref,
                   q_ref, k_ref, u_ref, vt_ref):
    def col(c):
        return p_ref[:, c * LANES:(c + 1) * LANES]

    def put(ref, c, x):
        ref[:, c * LANES:(c + 1) * LANES] = x.astype(BF16)

    is_tail = pl.program_id(0) == pl.num_programs(0) - 1

    def put_t(c, x):
        xt = x.T
        vt_ref[0, c * LANES:(c + 1) * LANES, :] = jnp.where(is_tail, jnp.zeros_like(xt), xt).astype(BF16)

    rope_a = lambda x: _rope(x, ca_ref[...], ua_ref[...], da_ref[...], HEAD_DIM_A // 4)
    rope_b = lambda x: _rope(x, cb_ref[...], ub_ref[...], db_ref[...], HEAD_DIM_B // 4)
    scale_a, scale_b = LOG2_E * HEAD_DIM_A ** -0.5, LOG2_E * HEAD_DIM_B ** -0.5
    c0 = 0
    for h in range(N_HEADS_A):
        put(q_ref, h, rope_a(_head_rmsnorm(col(c0 + h), gq_ref[...])) * scale_a)
    c0 += N_HEADS_A
    for h in range(N_HEADS_B):
        put(q_ref, N_HEADS_A + h, rope_b(col(c0 + h)) * scale_b)
    c0 += N_HEADS_B
    for g in range(N_FOURIER_GROUPS):
        put(u_ref, g, col(c0 + g))
    c0 += N_FOURIER_GROUPS
    for h in range(N_KV_HEADS_A):
        put(k_ref, h, rope_a(_head_rmsnorm(col(c0 + h), gk_ref[...])))
    c0 += N_KV_HEADS_A
    for h in range(N_KV_HEADS_A):
        put_t(h, col(c0 + h))
    c0 += N_KV_HEADS_A
    for h in range(N_HEADS_B):
        put(k_ref, N_KV_HEADS_A + h, rope_b(col(c0 + h)))
    c0 += N_HEADS_B
    for h in range(N_HEADS_B):
        put_t(N_KV_HEADS_A + h, col(c0 + h))


Q_COLS = QA_COLS + QB_COLS
K_COLS = KA_COLS + KB_COLS
V_COLS = VA_COLS + VB_COLS


def _mixer_front_kernel(x_ref, g_ref, mod_ref, w_ref, *rest):
    p_ref = rest[-1]
    h = _norm_modulate(x_ref[...], g_ref[...], mod_ref[0], 0)
    p_ref[...] = jnp.dot(h.astype(BF16), w_ref[...], preferred_element_type=F32)
    _qkpost_kernel(p_ref, *rest[:-1])


def _mixer_front(x, g, mod, w_in_bf16, g_q, g_k, rope_a, rope_b, n_lat):
    t, d = x.shape
    n_tiles = t // ROW_TILE
    n_lat_tiles = n_lat // ROW_TILE
    tiles_per_chunk = ATTN_TK // ROW_TILE
    assert tiles_per_chunk == 2 and (n_tiles + 1) % tiles_per_chunk == 0
    row = lambda w: pl.BlockSpec((ROW_TILE, w), lambda i: (jnp.minimum(i, n_tiles - 1), 0))
    vec = lambda w: pl.BlockSpec((1, w), lambda i: (0, 0))
    vt_spec = pl.BlockSpec((1, V_COLS, ROW_TILE), lambda i: (i // tiles_per_chunk, 0, i % tiles_per_chunk))
    return pl.pallas_call(
        _mixer_front_kernel,
        grid=(n_tiles + 1,),
        in_specs=[row(d), vec(d), _mod_spec(d, n_lat_tiles), pl.BlockSpec((d, IN_COLS), lambda i: (0, 0)),
                  vec(LANES), vec(LANES)] + [row(LANES)] * 6,
        out_specs=[row(Q_COLS), row(K_COLS), row(C_COLS), vt_spec],
        out_shape=[jax.ShapeDtypeStruct((t, Q_COLS), BF16), jax.ShapeDtypeStruct((t, K_COLS), BF16),
                   jax.ShapeDtypeStruct((t, C_COLS), BF16),
                   jax.ShapeDtypeStruct((n_lat // ATTN_TK + 1, V_COLS, ATTN_TK), BF16)],
        scratch_shapes=[pltpu.VMEM((ROW_TILE, IN_COLS), F32)],
        compiler_params=_params("arbitrary"),
        name="mixer_front",
    )(x, g.reshape(1, d), mod, w_in_bf16, g_q.reshape(1, LANES), g_k.reshape(1, LANES), *rope_a, *rope_b)


ATTN_TQ = 256
ATTN_TK = 512


REDUCE_ROWS = 32


def _stream_reduce(op, x):
    rows = x.shape[0]
    step = REDUCE_ROWS if rows % REDUCE_ROWS == 0 else rows
    acc = x[:step]
    for r in range(step, rows, step):
        acc = op(acc, x[r:r + step])
    red = jnp.max if op is jnp.maximum else jnp.sum
    return red(acc, axis=0, keepdims=True)


ONES_ROWS = 16


class _Tile:
    def __init__(self, refs, kv):
        self.qt, self.m, self.l, self.acc, s_a, c_a, s_b, c_b = refs
        self.stage = ((s_a, c_a), (s_b, c_b))
        self.kv = slice(kv * LANES, (kv + 1) * LANES)


def _attn_scratch(n_tiles):
    stat = pltpu.VMEM((1, ATTN_TQ), F32)
    score = pltpu.VMEM((ATTN_TK, ATTN_TQ), F32)
    per_tile = [pltpu.VMEM((LANES, ATTN_TQ), BF16), stat, stat, pltpu.VMEM((LANES, ATTN_TQ), F32),
                score, stat, score, stat]
    return per_tile * n_tiles


def _make_tiles(scratch, kv_of_tile):
    n = len(scratch) // len(kv_of_tile)
    return [_Tile(scratch[n * i:n * (i + 1)], kv) for i, kv in enumerate(kv_of_tile)]


def _flash_t(tiles, k_ref, vt_ref, *, n_lat, n_ctx, latent):
    n_chunks = n_lat // ATTN_TK
    for t in tiles:
        t.m[...] = jnp.full(t.m.shape, NEG_BIG, F32)
        t.l[...] = jnp.zeros(t.l.shape, F32)
        t.acc[...] = jnp.zeros(t.acc.shape, F32)

    def scores(t, k, stage, rows):
        s = jnp.dot(k, t.qt[...], preferred_element_type=F32)
        s_ref, c_ref = t.stage[stage]
        s_ref[0:rows, :] = s
        c_ref[...] = _stream_reduce(jnp.maximum, s)

    def scores_lat(j, stage):
        off = pl.multiple_of(j * ATTN_TK, ATTN_TK)
        for t in tiles:
            scores(t, k_ref[pl.ds(off, ATTN_TK), t.kv], stage, ATTN_TK)

    def scores_ctx():
        for t in tiles:
            scores(t, k_ref[n_lat:n_lat + n_ctx, t.kv], 0, n_ctx)

    def consume(t, vt, stage, rows):
        s_ref, c_ref = t.stage[stage]
        m_old = t.m[...]
        m_new = jnp.maximum(m_old, c_ref[...])
        alpha = jnp.exp2(m_old - m_new)
        p = jnp.exp2(s_ref[0:rows, :] - m_new).astype(BF16)
        vt_ones = jnp.concatenate([vt, jnp.ones((ONES_ROWS, rows), BF16)], axis=0)
        pv = jnp.dot(vt_ones, p, preferred_element_type=F32)
        t.l[...] = alpha * t.l[...] + pv[LANES:LANES + 1]
        t.acc[...] = alpha * t.acc[...] + pv[:LANES]
        t.m[...] = m_new

    def consume_lat(j, stage):
        for t in tiles:
            consume(t, vt_ref[j, t.kv, :], stage, ATTN_TK)

    if latent:
        scores_lat(0, 0)

        def body(i, carry):
            j = 2 * i
            scores_lat(j + 1, 1)
            consume_lat(j, 0)
            scores_lat(j + 2, 0)
            consume_lat(j + 1, 1)
            return carry
        lax.fori_loop(0, n_chunks // 2 - 1, body, 0)
        scores_lat(n_chunks - 1, 1)
        consume_lat(n_chunks - 2, 0)
        scores_ctx()
        consume_lat(n_chunks - 1, 1)
    else:
        scores_ctx()
    for t in tiles:
        consume(t, vt_ref[n_chunks, t.kv, 0:n_ctx], 0, n_ctx)


def _kv_specs(t, n_chunks, width, first_blk):
    return [pl.BlockSpec((t, width), lambda g, i: (0, first_blk + g)),
            pl.BlockSpec((n_chunks + 1, width, ATTN_TK), lambda g, i: (0, first_blk + g, 0))]


LATENT_ROW_BLOCKS = 2


def _row_block(rb):
    return slice(rb * ATTN_TQ, (rb + 1) * ATTN_TQ)


def _gqa_kernel(q_ref, k_ref, vt_ref, o_ref, *scratch, n_lat, n_ctx, latent, row_blocks):
    group = N_HEADS_A // N_KV_HEADS_A
    tiles = _make_tiles(scratch, [0] * (row_blocks * group))
    for rb in range(row_blocks):
        for h in range(group):
            tiles[rb * group + h].qt[...] = (
                q_ref[_row_block(rb), h * LANES:(h + 1) * LANES].astype(F32).T.astype(BF16))
    _flash_t(tiles, k_ref, vt_ref, n_lat=n_lat, n_ctx=n_ctx, latent=latent)
    for rb in range(row_blocks):
        for h in range(group):
            t = tiles[rb * group + h]
            o_ref[_row_block(rb), h * LANES:(h + 1) * LANES] = (t.acc[...] / t.l[...]).T.astype(o_ref.dtype)


def _attn_rows(n_lat, n_ctx, latent):
    if latent:
        assert n_lat % (LATENT_ROW_BLOCKS * ATTN_TQ) == 0
        return LATENT_ROW_BLOCKS, n_lat // (LATENT_ROW_BLOCKS * ATTN_TQ), 0
    assert n_ctx == ATTN_TQ
    return 1, 1, n_lat // ATTN_TQ


def _gqa_attention(q, k, vt, n_lat, n_ctx, latent):
    group = N_HEADS_A // N_KV_HEADS_A
    row_blocks, steps, first = _attn_rows(n_lat, n_ctx, latent)
    rows = row_blocks * ATTN_TQ
    return pl.pallas_call(
        functools.partial(_gqa_kernel, n_lat=n_lat, n_ctx=n_ctx, latent=latent, row_blocks=row_blocks),
        grid=(N_KV_HEADS_A, steps),
        in_specs=[pl.BlockSpec((rows, group * LANES), lambda g, i: (first + i, g))]
        + _kv_specs(n_lat + n_ctx, n_lat // ATTN_TK, LANES, 0),
        out_specs=pl.BlockSpec((rows, group * LANES), lambda g, i: (i, g)),
        out_shape=jax.ShapeDtypeStruct((steps * rows, QA_COLS), BF16),
        scratch_shapes=_attn_scratch(row_blocks * group),
        compiler_params=_params("arbitrary", "arbitrary"),
        name="gqa_attention" if latent else "gqa_attention_ctx",
    )(q, k, vt)


DIFF_HEADS_PER_STEP = 2


def _diff_kernel(q_ref, k_ref, vt_ref, lam_ref, gs_ref, o_ref, *scratch,
                 n_lat, n_ctx, lambda_init, latent, row_blocks):
    heads = DIFF_HEADS_PER_STEP
    tiles = _make_tiles(scratch, [h for _ in range(row_blocks) for h in range(heads) for _ in range(2)])
    lane = lax.broadcasted_iota(jnp.int32, (ATTN_TQ, LANES), 1)
    for rb in range(row_blocks):
        for h in range(heads):
            qh = q_ref[_row_block(rb), h * LANES:(h + 1) * LANES].astype(F32)
            zero = jnp.zeros_like(qh)
            first = (rb * heads + h) * 2
            tiles[first].qt[...] = jnp.where(lane < HEAD_DIM_B, qh, zero).T.astype(BF16)
            tiles[first + 1].qt[...] = jnp.where(lane >= HEAD_DIM_B, qh, zero).T.astype(BF16)
    _flash_t(tiles, k_ref, vt_ref, n_lat=n_lat, n_ctx=n_ctx, latent=latent)
    lv = lam_ref[...]
    lam = (jnp.exp(jnp.sum(lv[0:1] * lv[1:2], axis=-1, keepdims=True))
           - jnp.exp(jnp.sum(lv[2:3] * lv[3:4], axis=-1, keepdims=True)) + lambda_init)
    for rb in range(row_blocks):
        for h in range(heads):
            first = (rb * heads + h) * 2
            out1, out2 = (t.acc[...] / t.l[...] for t in tiles[first:first + 2])
            diff = (out1 - lam * out2).T
            o_ref[_row_block(rb), h * LANES:(h + 1) * LANES] = (
                _head_rmsnorm(diff, gs_ref[...]) * (1.0 - lambda_init)).astype(o_ref.dtype)


def _diff_attention(q, k, vt, lam_vecs, g_subln, lambda_init, n_lat, n_ctx, latent):
    width = DIFF_HEADS_PER_STEP * LANES
    assert QA_COLS % width == 0 and KA_COLS % width == 0
    row_blocks, steps, first = _attn_rows(n_lat, n_ctx, latent)
    rows = row_blocks * ATTN_TQ
    return pl.pallas_call(
        functools.partial(_diff_kernel, n_lat=n_lat, n_ctx=n_ctx, lambda_init=lambda_init,
                          latent=latent, row_blocks=row_blocks),
        grid=(N_HEADS_B // DIFF_HEADS_PER_STEP, steps),
        in_specs=[pl.BlockSpec((rows, width), lambda h, i: (first + i, QA_COLS // width + h))]
        + _kv_specs(n_lat + n_ctx, n_lat // ATTN_TK, width, KA_COLS // width)
        + [pl.BlockSpec((4, HEAD_DIM_B), lambda h, i: (0, 0)),
           pl.BlockSpec((1, LANES), lambda h, i: (0, 0))],
        out_specs=pl.BlockSpec((rows, width), lambda h, i: (i, h)),
        out_shape=jax.ShapeDtypeStruct((steps * rows, VB_COLS), BF16),
        scratch_shapes=_attn_scratch(row_blocks * DIFF_HEADS_PER_STEP * 2),
        compiler_params=_params("arbitrary", "arbitrary"),
        name="diff_attention" if latent else "diff_attention_ctx",
    )(q, k, vt, lam_vecs, g_subln.reshape(1, LANES))


def _dft_cos_sin(n):
    j = np.arange(n, dtype=np.float64)
    ang = 2.0 * np.pi * np.outer(j, j) / n
    return np.cos(ang), np.sin(ang)


def _channel_mix_weights(cs_ref, wf_ref, ab_ref):
    cs = cs_ref[...].astype(BF16)
    for g in range(N_FOURIER_GROUPS):
        ab_ref[g] = jnp.dot(cs, wf_ref[g].astype(BF16), preferred_element_type=F32).astype(BF16)


def _channel_mix(xr, xi, ab_ref, norm):
    outs = []
    for g in range(N_FOURIER_GROUPS):
        sl = slice(g * LANES, (g + 1) * LANES)
        lhs = jnp.concatenate([xr[:, sl], xi[:, sl]], axis=-1).astype(BF16)
        outs.append(jnp.dot(lhs, ab_ref[g], preferred_element_type=F32))
    return jnp.concatenate(outs, axis=-1) * norm


def _fourier_stage1_kernel(w_ref, u_ref, o_ref):
    o_ref[...] = jnp.dot(w_ref[...].astype(BF16), u_ref[...],
                         preferred_element_type=F32).astype(o_ref.dtype)


def _fourier_stage2_kernel(a_ref, tc_ref, ts_ref, w3_ref, cs_ref, wf_ref, o_ref, ab_ref, *, norm):
    @pl.when(pl.program_id(0) == 0)
    def _():
        _channel_mix_weights(cs_ref, wf_ref, ab_ref)

    ar = a_ref[0, 0].astype(F32)
    ai = a_ref[1, 0].astype(F32)
    tc, ts = tc_ref[0], ts_ref[0]
    br = ar * tc + ai * ts
    bi = ai * tc - ar * ts
    stack = jnp.concatenate([br, bi], axis=0).astype(BF16)
    x = jnp.dot(w3_ref[...].astype(BF16), stack, preferred_element_type=F32)
    half = x.shape[0] // 2
    o_ref[...] = _channel_mix(x[:half], x[half:], ab_ref, norm).astype(o_ref.dtype)


def _fourier_ctx_kernel(u_ref, w_ref, cs_ref, wf_ref, o_ref, ab_ref, *, norm):
    _channel_mix_weights(cs_ref, wf_ref, ab_ref)
    x = jnp.dot(w_ref[...].astype(BF16), u_ref[...], preferred_element_type=F32)
    half = x.shape[0] // 2
    o_ref[...] = _channel_mix(x[:half], x[half:], ab_ref, norm).astype(o_ref.dtype)


def _fourier_latent(u, n, w_fourier):
    assert u.shape[0] % LANES == 0
    n2 = LANES
    n1 = n // n2
    c1, s1 = _dft_cos_sin(n1)
    w1 = jnp.asarray(np.concatenate([c1, -s1], axis=0), F32)
    k1n2 = 2.0 * np.pi * np.outer(np.arange(n1), np.arange(n2)) / n
    tw_c = jnp.asarray(np.cos(k1n2)[:, :, None], F32)
    tw_s = jnp.asarray(np.sin(k1n2)[:, :, None], F32)
    c2, s2 = _dft_cos_sin(n2)
    w3 = jnp.asarray(np.block([[c2, s2], [-s2, c2]]), F32)
    cc, sc = _dft_cos_sin(FOURIER_GROUP_DIM)
    cs = jnp.asarray(np.concatenate([cc, sc], axis=0), F32)
    width = n2 * C_COLS
    cb = _largest_tile(width, (4096, 2048, 1024, 512))
    stage1 = pl.pallas_call(
        _fourier_stage1_kernel,
        grid=(width // cb,),
        in_specs=[pl.BlockSpec((2 * n1, n1), lambda j: (0, 0)),
                  pl.BlockSpec((n1, cb), lambda j: (0, j))],
        out_specs=pl.BlockSpec((2 * n1, cb), lambda j: (0, j)),
        out_shape=jax.ShapeDtypeStruct((2 * n1, width), BF16),
        compiler_params=_params("arbitrary"),
        name="fourier_stage1",
    )(w1, u.reshape(u.shape[0] // n2, width))
    a = stage1.reshape(2, n1, n2, C_COLS)
    norm = 1.0 / math.sqrt(n * FOURIER_GROUP_DIM)
    out = pl.pallas_call(
        functools.partial(_fourier_stage2_kernel, norm=norm),
        grid=(n1,),
        in_specs=[pl.BlockSpec((2, 1, n2, C_COLS), lambda k: (0, k, 0, 0)),
                  pl.BlockSpec((1, n2, 1), lambda k: (k, 0, 0)),
                  pl.BlockSpec((1, n2, 1), lambda k: (k, 0, 0)),
                  pl.BlockSpec((2 * n2, 2 * n2), lambda k: (0, 0)),
                  pl.BlockSpec((2 * FOURIER_GROUP_DIM, FOURIER_GROUP_DIM), lambda k: (0, 0)),
                  pl.BlockSpec((N_FOURIER_GROUPS, FOURIER_GROUP_DIM, FOURIER_GROUP_DIM),
                               lambda k: (0, 0, 0))],
        out_specs=pl.BlockSpec((n2, C_COLS), lambda k: (0, k)),
        out_shape=jax.ShapeDtypeStruct((n2, n1 * C_COLS), BF16),
        scratch_shapes=[pltpu.VMEM((N_FOURIER_GROUPS, 2 * FOURIER_GROUP_DIM, FOURIER_GROUP_DIM), BF16)],
        compiler_params=_params("arbitrary"),
        name="fourier_stage2",
    )(a, tw_c, tw_s, w3, cs, w_fourier)
    return out.reshape(n, C_COLS)


def _fourier_ctx(u_ctx, w_fourier):
    m = u_ctx.shape[0]
    c, s = _dft_cos_sin(m)
    w = jnp.asarray(np.concatenate([c, -s], axis=0), F32)
    cc, sc = _dft_cos_sin(FOURIER_GROUP_DIM)
    cs = jnp.asarray(np.concatenate([cc, sc], axis=0), F32)
    norm = 1.0 / math.sqrt(m * FOURIER_GROUP_DIM)
    return pl.pallas_call(
        functools.partial(_fourier_ctx_kernel, norm=norm),
        out_shape=jax.ShapeDtypeStruct((m, C_COLS), BF16),
        scratch_shapes=[pltpu.VMEM((N_FOURIER_GROUPS, 2 * FOURIER_GROUP_DIM, FOURIER_GROUP_DIM), BF16)],
        compiler_params=pltpu.CompilerParams(vmem_limit_bytes=VMEM_LIMIT),
        name="fourier_ctx",
    )(u_ctx, w, cs, w_fourier)


def _gated_residual(x, f, g, mod, gate_row):
    ms = jnp.mean(f * f, axis=-1, keepdims=True)
    return x + mod[gate_row:gate_row + 1] * (f * lax.rsqrt(ms + NORM_EPS) * g)


def _residual_kernel(x_ref, f_ref, g_ref, mod_ref, o_ref, *, gate_row):
    o_ref[...] = _gated_residual(x_ref[...], f_ref[...], g_ref[...], mod_ref[0], gate_row)


def _post_residual(x, f, g, mod, gate_row, n_rows, n_lat):
    d = x.shape[1]
    row = pl.BlockSpec((ROW_TILE, d), lambda i: (i, 0))
    return pl.pallas_call(
        functools.partial(_residual_kernel, gate_row=gate_row),
        grid=(n_rows // ROW_TILE,),
        in_specs=[row, row, pl.BlockSpec((1, d), lambda i: (0, 0)), _mod_spec(d, n_lat // ROW_TILE)],
        out_specs=row,
        out_shape=jax.ShapeDtypeStruct((n_rows, d), F32),
        compiler_params=_params("arbitrary"),
        name="post_residual",
    )(x, f, g.reshape(1, d), mod)


def _outproj_residual_kernel(*refs, n_pieces, gate_row):
    y_refs = refs[:n_pieces]
    w_ref, x_ref, g_ref, mod_ref, o_ref = refs[n_pieces:]
    y = jnp.concatenate([r[...] for r in y_refs], axis=-1)
    f = jnp.dot(y, w_ref[...], preferred_element_type=F32)
    o_ref[...] = _gated_residual(x_ref[...], f, g_ref[...], mod_ref[0], gate_row)


def _outproj_residual(y_pieces, w_bf16, x, g, mod, gate_row, n_rows, n_lat):
    d = x.shape[1]
    k = w_bf16.shape[0]
    assert sum(piece.shape[1] for piece in y_pieces) == k
    row = lambda w: pl.BlockSpec((ROW_TILE, w), lambda i: (i, 0))
    return pl.pallas_call(
        functools.partial(_outproj_residual_kernel, n_pieces=len(y_pieces), gate_row=gate_row),
        grid=(n_rows // ROW_TILE,),
        in_specs=[row(piece.shape[1]) for piece in y_pieces]
        + [pl.BlockSpec((k, d), lambda i: (0, 0)), row(d), pl.BlockSpec((1, d), lambda i: (0, 0)),
           _mod_spec(d, n_lat // ROW_TILE)],
        out_specs=row(d),
        out_shape=jax.ShapeDtypeStruct((n_rows, d), F32),
        compiler_params=_params("arbitrary"),
        name="out_proj_residual",
    )(*y_pieces, w_bf16, x, g.reshape(1, d), mod)


MOE_TM = 256
DISPATCH_TM = 2 * MOE_TM
R_IDX, R_W, R_RANK = 0, 2, 4


def _router_kernel(x_ref, g_ref, mod_ref, wr_ref, h_ref, route_ref, cnt_ref, carry_ref, *, shift_row):
    @pl.when(pl.program_id(0) == 0)
    def _():
        carry_ref[...] = jnp.zeros(carry_ref.shape, F32)

    h = _norm_modulate(x_ref[...], g_ref[...], mod_ref[0], shift_row)
    h_ref[...] = h
    logits = jnp.dot(h, wr_ref[...], preferred_element_type=F32, precision=lax.Precision.HIGHEST)
    rows = logits.shape[0]
    lane = lax.broadcasted_iota(jnp.int32, logits.shape, 1).astype(F32)
    neg = jnp.full(logits.shape, -jnp.inf, F32)
    l1 = jnp.where(lane < N_EXPERTS, logits, neg)
    m1 = jnp.max(l1, axis=-1, keepdims=True)
    i1 = jnp.min(jnp.where(l1 == m1, lane, float(LANES)), axis=-1, keepdims=True)
    l2 = jnp.where(lane == i1, neg, l1)
    m2 = jnp.max(l2, axis=-1, keepdims=True)
    i2 = jnp.min(jnp.where(l2 == m2, lane, float(LANES)), axis=-1, keepdims=True)
    e2 = jnp.exp(m2 - m1)
    w1 = 1.0 / (1.0 + e2)
    w2 = e2 / (1.0 + e2)
    hit1 = lane == i1
    hit2 = lane == i2
    onehot = jnp.logical_or(hit1, hit2).astype(BF16)
    r_i = lax.broadcasted_iota(jnp.int32, (rows, rows), 0)
    c_i = lax.broadcasted_iota(jnp.int32, (rows, rows), 1)
    before = (c_i < r_i).astype(BF16)
    base = carry_ref[...] + jnp.dot(before, onehot, preferred_element_type=F32)
    rank1 = jnp.sum(jnp.where(hit1, base, 0.0), axis=-1, keepdims=True)
    rank2 = jnp.sum(jnp.where(hit2, base, 0.0), axis=-1, keepdims=True)
    carry_ref[...] = carry_ref[...] + jnp.sum(onehot.astype(F32), axis=0, keepdims=True)
    rec = jnp.zeros(logits.shape, F32)
    for off, val in ((R_IDX, i1), (R_IDX + 1, i2), (R_W, w1), (R_W + 1, w2),
                     (R_RANK, rank1), (R_RANK + 1, rank2)):
        rec = jnp.where(lane == off, val, rec)
    route_ref[...] = rec
    cnt_ref[...] = jnp.broadcast_to(carry_ref[...], cnt_ref.shape)


def _router(x, g, mod, w_router, n_lat):
    d = x.shape[1]
    wr = jnp.zeros((d, LANES), F32).at[:, :N_EXPERTS].set(w_router)
    return pl.pallas_call(
        functools.partial(_router_kernel, shift_row=3),
        grid=(n_lat // ROW_TILE,),
        in_specs=[pl.BlockSpec((ROW_TILE, d), lambda i: (i, 0)),
                  pl.BlockSpec((1, d), lambda i: (0, 0)),
                  pl.BlockSpec((1, N_MOD, d), lambda i: (0, 0, 0)),
                  pl.BlockSpec((d, LANES), lambda i: (0, 0))],
        out_specs=[pl.BlockSpec((ROW_TILE, d), lambda i: (i, 0)),
                   pl.BlockSpec((ROW_TILE, LANES), lambda i: (i, 0)),
                   pl.BlockSpec((8, LANES), lambda i: (0, 0))],
        out_shape=[jax.ShapeDtypeStruct((n_lat, d), F32),
                   jax.ShapeDtypeStruct((n_lat, LANES), F32),
                   jax.ShapeDtypeStruct((8, LANES), F32)],
        scratch_shapes=[pltpu.VMEM((1, LANES), F32)],
        compiler_params=_params("arbitrary"),
        name="router",
    )(x, g.reshape(1, d), mod, wr)


def _row_copy(src_hbm, src_row, dst_ref, dst_row, sem):
    return pltpu.make_async_copy(src_hbm.at[pl.ds(src_row, 1)], dst_ref.at[pl.ds(dst_row, 1)], sem)


def _wait_rows(src_hbm, dst_ref, sem):
    pltpu.make_async_copy(src_hbm.at[pl.ds(0, dst_ref.shape[0])], dst_ref, sem).wait()


ISSUE_UNROLL = 8


def _dispatch_kernel(src_ref, nv_ref, h_hbm, o_ref, buf_ref, sems):
    i = pl.program_id(0)
    n_valid = nv_ref[0]
    slot = lax.rem(i, 2)

    def issue(tile, slot):
        base = tile * DISPATCH_TM

        def body(r, carry):
            _row_copy(h_hbm, src_ref[base + r], buf_ref.at[slot], r, sems.at[slot]).start()
            return carry
        lax.fori_loop(0, DISPATCH_TM, body, 0, unroll=ISSUE_UNROLL)

    @pl.when(jnp.logical_and(i == 0, n_valid > 0))
    def _():
        issue(0, 0)

    @pl.when(i + 1 < n_valid)
    def _():
        issue(i + 1, 1 - slot)

    @pl.when(i < n_valid)
    def _():
        _wait_rows(h_hbm, buf_ref.at[slot], sems.at[slot])
        o_ref[...] = buf_ref[slot].astype(o_ref.dtype)

    @pl.when(i >= n_valid)
    def _():
        o_ref[...] = jnp.zeros(o_ref.shape, o_ref.dtype)


def _dispatch(src_token, n_valid, h):
    d = h.shape[1]
    n_slots = src_token.shape[0]
    grid_spec = pltpu.PrefetchScalarGridSpec(
        num_scalar_prefetch=2,
        grid=(n_slots // DISPATCH_TM,),
        in_specs=[pl.BlockSpec(memory_space=pl.ANY)],
        out_specs=pl.BlockSpec((DISPATCH_TM, d), lambda i, src, nv: (i, 0)),
        scratch_shapes=[pltpu.VMEM((2, DISPATCH_TM, d), F32), pltpu.SemaphoreType.DMA((2,))],
    )
    return pl.pallas_call(
        _dispatch_kernel,
        grid_spec=grid_spec,
        out_shape=jax.ShapeDtypeStruct((n_slots, d), BF16),
        compiler_params=_params("arbitrary"),
        name="moe_dispatch",
    )(src_token, n_valid, h)


def _combine_kernel(dest_ref, y_hbm, route_ref, x_ref, g_ref, mod_ref, o_ref, buf_ref, sems, *, gate_row):
    i = pl.program_id(0)
    slot = lax.rem(i, 2)

    def issue(tile, slot):
        base = tile * ROW_TILE

        def body(r, carry):
            for k in range(TOP_K):
                _row_copy(y_hbm, dest_ref[TOP_K * (base + r) + k], buf_ref.at[slot], k * ROW_TILE + r,
                          sems.at[slot]).start()
            return carry
        lax.fori_loop(0, ROW_TILE, body, 0, unroll=ISSUE_UNROLL)

    @pl.when(i == 0)
    def _():
        issue(0, 0)

    @pl.when(i + 1 < pl.num_programs(0))
    def _():
        issue(i + 1, 1 - slot)

    _wait_rows(y_hbm, buf_ref.at[slot], sems.at[slot])
    route = route_ref[...]
    f = (route[:, R_W:R_W + 1] * buf_ref[slot, 0:ROW_TILE, :]
         + route[:, R_W + 1:R_W + 2] * buf_ref[slot, ROW_TILE:TOP_K * ROW_TILE, :])
    o_ref[...] = _gated_residual(x_ref[...], f, g_ref[...], mod_ref[0], gate_row)


def _combine_residual(dest_flat, y_sorted, route, x, g, mod, gate_row, n_lat):
    d = x.shape[1]
    row = lambda w: pl.BlockSpec((ROW_TILE, w), lambda i, dest: (i, 0))
    grid_spec = pltpu.PrefetchScalarGridSpec(
        num_scalar_prefetch=1,
        grid=(n_lat // ROW_TILE,),
        in_specs=[pl.BlockSpec(memory_space=pl.ANY), row(LANES), row(d),
                  pl.BlockSpec((1, d), lambda i, dest: (0, 0)),
                  pl.BlockSpec((1, N_MOD, d), lambda i, dest: (0, 0, 0))],
        out_specs=row(d),
        scratch_shapes=[pltpu.VMEM((2, TOP_K * ROW_TILE, d), F32), pltpu.SemaphoreType.DMA((2,))],
    )
    return pl.pallas_call(
        functools.partial(_combine_kernel, gate_row=gate_row),
        grid_spec=grid_spec,
        out_shape=jax.ShapeDtypeStruct((n_lat, d), F32),
        compiler_params=_params("arbitrary"),
        name="moe_combine",
    )(dest_flat, y_sorted, route, x, g.reshape(1, d), mod)


def _moe_layer(x, g_pre, g_post, mod, w_router, w_gate, w_up, w_down, n_lat):
    d_ff = w_gate.shape[-1]
    h, route, cnt = _router(x, g_pre, mod, w_router, n_lat)
    experts = route[:, R_IDX:R_IDX + TOP_K].astype(jnp.int32)
    ranks = route[:, R_RANK:R_RANK + TOP_K].astype(jnp.int32)
    counts = cnt[0, :N_EXPERTS].astype(jnp.int32)
    padded = ((counts + MOE_TM - 1) // MOE_TM) * MOE_TM
    ends = jnp.cumsum(padded)
    starts = ends - padded
    dest = (starts[experts] + ranks).reshape(-1)
    n_slots = TOP_K * n_lat + N_EXPERTS * MOE_TM
    assert n_slots % DISPATCH_TM == 0
    n_valid = ((ends[-1:] + DISPATCH_TM - 1) // DISPATCH_TM).astype(jnp.int32)

    token = jnp.repeat(jnp.arange(n_lat, dtype=jnp.int32), TOP_K)
    src_token = jnp.zeros((n_slots,), jnp.int32).at[dest].set(token)
    hs = _dispatch(src_token, n_valid, h)
    seg_start = jnp.concatenate([starts, ends[-1:]]).astype(jnp.int32)
    seg_tiles = (jnp.concatenate([padded, n_slots - ends[-1:]]) // MOE_TM).astype(jnp.int32)
    act = _expert_matmul(hs, [w_gate, w_up], seg_start, seg_tiles, tm=MOE_TM, group=4,
                         tn=_largest_tile(d_ff, (512, 256, 128)), out_dtype=BF16, swiglu=True,
                         name="moe_gate_up")
    y = _expert_matmul(act, [w_down], seg_start, seg_tiles, tm=MOE_TM, group=1,
                       tn=_largest_tile(x.shape[1], (512, 256, 128)), out_dtype=F32, name="moe_down")
    return _combine_residual(dest, y, route, x, g_post, mod, 5, n_lat)


def kernel(x, c, ctx, c_ctx, w_mod, b_mod, g_pre_mix, g_post_mix, g_pre_ffn, g_post_ffn, w_in, g_q_a, g_k_a, lambda_q1, lambda_k1, lambda_q2, lambda_k2, g_subln_b, w_fourier, w_out, w_gate_dense, w_up_dense, w_down_dense, w_router, w_gate_moe, w_up_moe, w_down_moe):
    batch, n_lat, d = x.shape
    n_ctx = ctx.shape[1]
    depth = w_mod.shape[0]
    assert batch == 1 and n_ctx == ROW_TILE and n_lat % max(2 * ATTN_TK, MOE_TM, GRID_W) == 0
    assert depth == 2, "layer 0 dense, layer 1 (last) routed experts"
    t = n_lat + n_ctx

    cond_rows = jnp.concatenate([c, c_ctx[None, :], jnp.zeros((6, d), F32)], axis=0)
    mod_all = _modulation(cond_rows, w_mod, b_mod)
    rope_a = _rope_tables(n_lat, n_ctx, HEAD_DIM_A, 1)
    rope_b = _rope_tables(n_lat, n_ctx, HEAD_DIM_B, 2)

    xs = jnp.concatenate([x[0], ctx[0]], axis=0)
    for layer in range(depth):
        last = layer == depth - 1
        n_out = n_lat if last else t
        mod = mod_all[layer, :2].reshape(2, N_MOD, d)
        lambda_init = 0.8 - 0.6 * math.exp(-0.3 * layer)

        q, k, u, vt = _mixer_front(xs, g_pre_mix[layer], mod, w_in[layer].astype(BF16),
                                   g_q_a[layer], g_k_a[layer], rope_a, rope_b, n_lat)
        lam_vecs = jnp.stack([lambda_q1[layer], lambda_k1[layer], lambda_q2[layer], lambda_k2[layer]])
        mixers = [[_gqa_attention(q, k, vt, n_lat, n_ctx, latent),
                   _diff_attention(q, k, vt, lam_vecs, g_subln_b[layer], lambda_init, n_lat, n_ctx, latent),
                   _fourier_latent(u, n_lat, w_fourier[layer]) if latent
                   else _fourier_ctx(u[n_lat:], w_fourier[layer])]
                  for latent in ((True,) if last else (True, False))]
        y_a, y_b, y_c = (jnp.concatenate(parts, axis=0) for parts in zip(*mixers))
        xs = _outproj_residual([y_a, y_b, y_c], w_out[layer].astype(BF16), xs, g_post_mix[layer], mod, 2,
                               n_out, n_lat)

        j = layer // 2
        if layer % 2 == 0:
            h = _prenorm(xs, g_pre_ffn[layer], mod, 3, n_out, n_lat)
            act = _dense_matmul(h, w_gate_dense[j], w2=w_up_dense[j], swiglu=True, tn=512,
                                out_dtype=BF16, name="ffn_gate_up")
            f = _dense_matmul(act, w_down_dense[j], tn=512, tiles=(ROW_TILE,), out_dtype=F32, name="ffn_down")
            xs = _post_residual(xs, f, g_post_ffn[layer], mod, 5, n_out, n_lat)
        else:
            xs = _moe_layer(xs, g_pre_ffn[layer], g_post_ffn[layer], mod, w_router[j],
                            w_gate_moe[j], w_up_moe[j], w_down_moe[j], n_lat)
    return xs[:n_lat][None]
```

```python
import functools
import math

import numpy as np
import jax
import jax.numpy as jnp
from jax import lax
from jax.experimental import pallas as pl
from jax.experimental.pallas import tpu as pltpu

F32 = jnp.float32
BF16 = jnp.bfloat16

GRID_W = 64
ROPE_THETA = 10000.0
NORM_EPS = 1e-6
N_HEADS_A, N_KV_HEADS_A, HEAD_DIM_A = 8, 2, 128
N_HEADS_B, HEAD_DIM_B = 4, 64
VALUE_DIM_B = 2 * HEAD_DIM_B
N_FOURIER_GROUPS, FOURIER_GROUP_DIM = 4, 128
QA_COLS = N_HEADS_A * HEAD_DIM_A
QB_COLS = N_HEADS_B * 2 * HEAD_DIM_B
C_COLS = N_FOURIER_GROUPS * FOURIER_GROUP_DIM
KA_COLS = N_KV_HEADS_A * HEAD_DIM_A
VA_COLS = KA_COLS
KB_COLS = QB_COLS
VB_COLS = N_HEADS_B * VALUE_DIM_B
KV_START = QA_COLS + QB_COLS + C_COLS
IN_COLS = KV_START + KA_COLS + VA_COLS + KB_COLS + VB_COLS
N_EXPERTS, TOP_K = 8, 2
N_MOD = 6

LANES = 128
V7X_VMEM_BYTES = 64 * 1024 * 1024
VMEM_LIMIT = 52 * 1024 * 1024

ROW_TILE = 256
NEG_BIG = -1e30
LOG2_E = math.log2(math.e)


def _params(*sem):
    return pltpu.CompilerParams(dimension_semantics=sem, vmem_limit_bytes=VMEM_LIMIT)


def _silu(x):
    return x / (1.0 + jnp.exp(-x))


def _largest_tile(n, candidates):
    for c in candidates:
        if n % c == 0:
            return c
    raise ValueError(f"no tile in {candidates} divides {n}")


def _mod_kernel(c_ref, w_ref, b_ref, o_ref):
    cond = _silu(c_ref[...])
    o_ref[0] = jnp.dot(cond.astype(BF16), w_ref[0].astype(BF16),
                       preferred_element_type=F32) + b_ref[0]


def _modulation(cond_rows, w_mod, b_mod):
    depth, d, n6 = w_mod.shape
    tn = _largest_tile(n6, (1024, 512, 256, 128))
    return pl.pallas_call(
        _mod_kernel,
        grid=(depth, n6 // tn),
        in_specs=[pl.BlockSpec((8, d), lambda l, j: (0, 0)),
                  pl.BlockSpec((1, d, tn), lambda l, j: (l, 0, j)),
                  pl.BlockSpec((1, 1, tn), lambda l, j: (l, 0, j))],
        out_specs=pl.BlockSpec((1, 8, tn), lambda l, j: (l, 0, j)),
        out_shape=jax.ShapeDtypeStruct((depth, 8, n6), F32),
        compiler_params=_params("arbitrary", "arbitrary"),
        name="modulation",
    )(cond_rows, w_mod, b_mod.reshape(depth, 1, n6))


def _norm_modulate(x, g, mod, shift_row):
    ms = jnp.mean(x * x, axis=-1, keepdims=True)
    y = x * lax.rsqrt(ms + NORM_EPS) * g
    return y * (1.0 + mod[shift_row + 1:shift_row + 2]) + mod[shift_row:shift_row + 1]


def _prenorm_kernel(x_ref, g_ref, mod_ref, o_ref, *, shift_row):
    h = _norm_modulate(x_ref[...], g_ref[...], mod_ref[0], shift_row)
    o_ref[...] = h.astype(o_ref.dtype)


def _mod_spec(d, n_lat_tiles):
    return pl.BlockSpec((1, N_MOD, d), lambda i: ((i >= n_lat_tiles).astype(jnp.int32), 0, 0))


def _prenorm(x, g, mod, shift_row, n_rows, n_lat):
    d = x.shape[1]
    return pl.pallas_call(
        functools.partial(_prenorm_kernel, shift_row=shift_row),
        grid=(n_rows // ROW_TILE,),
        in_specs=[pl.BlockSpec((ROW_TILE, d), lambda i: (i, 0)),
                  pl.BlockSpec((1, d), lambda i: (0, 0)),
                  _mod_spec(d, n_lat // ROW_TILE)],
        out_specs=pl.BlockSpec((ROW_TILE, d), lambda i: (i, 0)),
        out_shape=jax.ShapeDtypeStruct((n_rows, d), BF16),
        compiler_params=_params("arbitrary"),
        name="prenorm",
    )(x, g.reshape(1, d), mod)


def _gmm_kernel(tg_ref, rows_ref, lhs_ref, *refs, n_w, swiglu):
    w_refs, (o_ref, wb_ref) = refs[:n_w], refs[n_w:]
    m = pl.program_id(1)
    first = jnp.logical_or(m == 0, tg_ref[m] != tg_ref[jnp.maximum(m - 1, 0)])

    @pl.when(first)
    def _():
        for i in range(n_w):
            wb_ref[i] = w_refs[i][0].astype(BF16)

    live = rows_ref[m] > 0

    @pl.when(live)
    def _():
        x = lhs_ref[...]
        if swiglu:
            gate = jnp.dot(x, wb_ref[0], preferred_element_type=F32)
            up = jnp.dot(x, wb_ref[1], preferred_element_type=F32)
            out = _silu(gate) * up
        else:
            out = jnp.dot(x, wb_ref[0], preferred_element_type=F32)
        o_ref[...] = out.astype(o_ref.dtype)

    @pl.when(jnp.logical_not(live))
    def _():
        o_ref[...] = jnp.zeros(o_ref.shape, o_ref.dtype)


def _gmm(lhs, weights, tile_group, tile_rows, *, tm, tn, out_dtype, swiglu=False, n_rows=None, name="gmm"):
    n_rows = lhs.shape[0] if n_rows is None else n_rows
    _, k, nw = weights[0].shape
    n_w = len(weights)
    grid_spec = pltpu.PrefetchScalarGridSpec(
        num_scalar_prefetch=2,
        grid=(nw // tn, n_rows // tm),
        in_specs=[pl.BlockSpec((tm, k), lambda n, m, tg, tr: (m, 0))]
        + [pl.BlockSpec((1, k, tn), lambda n, m, tg, tr: (tg[m], 0, n)) for _ in range(n_w)],
        out_specs=pl.BlockSpec((tm, tn), lambda n, m, tg, tr: (m, n)),
        scratch_shapes=[pltpu.VMEM((n_w, k, tn), BF16)],
    )
    return pl.pallas_call(
        functools.partial(_gmm_kernel, n_w=n_w, swiglu=swiglu),
        grid_spec=grid_spec,
        out_shape=jax.ShapeDtypeStruct((n_rows, nw), out_dtype),
        compiler_params=_params("arbitrary", "arbitrary"),
        name=name,
    )(tile_group, tile_rows, lhs, *weights)


def _dense_matmul(lhs, w, *, tn, out_dtype, tiles=(1024, 768, 512, 256), swiglu=False, w2=None,
                  n_rows=None, name="matmul"):
    n_rows = lhs.shape[0] if n_rows is None else n_rows
    tm = _largest_tile(n_rows, tiles)
    n_tiles = n_rows // tm
    weights = [w[None]] if w2 is None else [w[None], w2[None]]
    return _gmm(lhs, weights, jnp.zeros((n_tiles,), jnp.int32), jnp.full((n_tiles,), tm, jnp.int32),
                tm=tm, tn=tn, out_dtype=out_dtype, swiglu=swiglu, n_rows=n_rows, name=name)


def _rope_tables(n_lat, n_ctx, head_dim, reps):
    n_rows_grid = n_lat // GRID_W
    rows = np.repeat(np.arange(n_rows_grid), GRID_W).astype(np.float32)
    cols = np.tile(np.arange(GRID_W), n_rows_grid).astype(np.float32)
    axis_dim = head_dim // 2
    freqs = (np.float32(ROPE_THETA) ** (-np.arange(0, axis_dim, 2, dtype=np.float32) / np.float32(axis_dim))
             ).astype(np.float32)
    ra, ca = rows[:, None] * freqs, cols[:, None] * freqs
    rc, rs, cc, cs = np.cos(ra), np.sin(ra), np.cos(ca), np.sin(ca)
    z = np.zeros_like(rs)
    cos = np.concatenate([rc, rc, cc, cc], axis=-1)
    sin_up = np.concatenate([-rs, z, -cs, z], axis=-1)
    sin_dn = np.concatenate([z, rs, z, cs], axis=-1)
    pad = lambda t, v: jnp.asarray(np.concatenate(
        [np.tile(t, (1, reps)), np.full((n_ctx, LANES), v, np.float32)], axis=0), F32)
    return pad(cos, 1.0), pad(sin_up, 0.0), pad(sin_dn, 0.0)


def _rope(x, cos, sin_up, sin_dn, quarter):
    return (x * cos + pltpu.roll(x, LANES - quarter, 1) * sin_up
            + pltpu.roll(x, quarter, 1) * sin_dn)


def _head_rmsnorm(x, g):
    return x * lax.rsqrt(jnp.mean(x * x, axis=-1, keepdims=True) + NORM_EPS) * g


def _qkpost_kernel(p_ref, gq_ref, gk_ref, ca_ref, ua_ref, da_ref, cb_ref, ub_ref, db_ref,
                   q_ref, k_ref, u_ref, vt_ref):
    def col(c):
        return p_ref[:, c * LANES:(c + 1) * LANES]

    def put(ref, c, x):
        ref[:, c * LANES:(c + 1) * LANES] = x.astype(BF16)

    is_tail = pl.program_id(0) == pl.num_programs(0) - 1

    def put_t(c, x):
        xt = x.T
        vt_ref[0, c * LANES:(c + 1) * LANES, :] = jnp.where(is_tail, jnp.zeros_like(xt), xt).astype(BF16)

    rope_a = lambda x: _rope(x, ca_ref[...], ua_ref[...], da_ref[...], HEAD_DIM_A // 4)
    rope_b = lambda x: _rope(x, cb_ref[...], ub_ref[...], db_ref[...], HEAD_DIM_B // 4)
    scale_a, scale_b = LOG2_E * HEAD_DIM_A ** -0.5, LOG2_E * HEAD_DIM_B ** -0.5
    c0 = 0
    for h in range(N_HEADS_A):
        put(q_ref, h, rope_a(_head_rmsnorm(col(c0 + h), gq_ref[...])) * scale_a)
    c0 += N_HEADS_A
    for h in range(N_HEADS_B):
        put(q_ref, N_HEADS_A + h, rope_b(col(c0 + h)) * scale_b)
    c0 += N_HEADS_B
    for g in range(N_FOURIER_GROUPS):
        put(u_ref, g, col(c0 + g))
    c0 += N_FOURIER_GROUPS
    for h in range(N_KV_HEADS_A):
        put(k_ref, h, rope_a(_head_rmsnorm(col(c0 + h), gk_ref[...])))
    c0 += N_KV_HEADS_A
    for h in range(N_KV_HEADS_A):
        put_t(h, col(c0 + h))
    c0 += N_KV_HEADS_A
    for h in range(N_HEADS_B):
        put(k_ref, N_KV_HEADS_A + h, rope_b(col(c0 + h)))
    c0 += N_HEADS_B
    for h in range(N_HEADS_B):
        put_t(N_KV_HEADS_A + h, col(c0 + h))


Q_COLS = QA_COLS + QB_COLS
K_COLS = KA_COLS + KB_COLS
V_COLS = VA_COLS + VB_COLS


def _mixer_front_kernel(x_ref, g_ref, mod_ref, w_ref, *rest):
    p_ref = rest[-1]
    h = _norm_modulate(x_ref[...], g_ref[...], mod_ref[0], 0)
    p_ref[...] = jnp.dot(h.astype(BF16), w_ref[...], preferred_element_type=F32)
    _qkpost_kernel(p_ref, *rest[:-1])


def _mixer_front(x, g, mod, w_in_bf16, g_q, g_k, rope_a, rope_b, n_lat):
    t, d = x.shape
    n_tiles = t // ROW_TILE
    n_lat_tiles = n_lat // ROW_TILE
    tiles_per_chunk = ATTN_TK // ROW_TILE
    assert tiles_per_chunk == 2 and (n_tiles + 1) % tiles_per_chunk == 0
    row = lambda w: pl.BlockSpec((ROW_TILE, w), lambda i: (jnp.minimum(i, n_tiles - 1), 0))
    vec = lambda w: pl.BlockSpec((1, w), lambda i: (0, 0))
    vt_spec = pl.BlockSpec((1, V_COLS, ROW_TILE), lambda i: (i // tiles_per_chunk, 0, i % tiles_per_chunk))
    return pl.pallas_call(
        _mixer_front_kernel,
        grid=(n_tiles + 1,),
        in_specs=[row(d), vec(d), _mod_spec(d, n_lat_tiles), pl.BlockSpec((d, IN_COLS), lambda i: (0, 0)),
                  vec(LANES), vec(LANES)] + [row(LANES)] * 6,
        out_specs=[row(Q_COLS), row(K_COLS), row(C_COLS), vt_spec],
        out_shape=[jax.ShapeDtypeStruct((t, Q_COLS), BF16), jax.ShapeDtypeStruct((t, K_COLS), BF16),
                   jax.ShapeDtypeStruct((t, C_COLS), BF16),
                   jax.ShapeDtypeStruct((n_lat // ATTN_TK + 1, V_COLS, ATTN_TK), BF16)],
        scratch_shapes=[pltpu.VMEM((ROW_TILE, IN_COLS), F32)],
        compiler_params=_params("arbitrary"),
        name="mixer_front",
    )(x, g.reshape(1, d), mod, w_in_bf16, g_q.reshape(1, LANES), g_k.reshape(1, LANES), *rope_a, *rope_b)


ATTN_TQ = 256
ATTN_TK = 512


REDUCE_ROWS = 32


def _stream_reduce(op, x):
    rows = x.shape[0]
    step = REDUCE_ROWS if rows % REDUCE_ROWS == 0 else rows
    acc = x[:step]
    for r in range(step, rows, step):
        acc = op(acc, x[r:r + step])
    red = jnp.max if op is jnp.maximum else jnp.sum
    return red(acc, axis=0, keepdims=True)


ONES_ROWS = 16


class _Tile:
    def __init__(self, refs, kv):
        self.qt, self.m, self.l, self.acc, s_a, c_a, s_b, c_b = refs
        self.stage = ((s_a, c_a), (s_b, c_b))
        self.kv = slice(kv * LANES, (kv + 1) * LANES)


def _attn_scratch(n_tiles):
    stat = pltpu.VMEM((1, ATTN_TQ), F32)
    score = pltpu.VMEM((ATTN_TK, ATTN_TQ), F32)
    per_tile = [pltpu.VMEM((LANES, ATTN_TQ), BF16), stat, stat, pltpu.VMEM((LANES, ATTN_TQ), F32),
                score, stat, score, stat]
    return per_tile * n_tiles


def _make_tiles(scratch, kv_of_tile):
    n = len(scratch) // len(kv_of_tile)
    return [_Tile(scratch[n * i:n * (i + 1)], kv) for i, kv in enumerate(kv_of_tile)]


def _flash_t(tiles, k_ref, vt_ref, *, n_lat, n_ctx, latent):
    n_chunks = n_lat // ATTN_TK
    for t in tiles:
        t.m[...] = jnp.full(t.m.shape, NEG_BIG, F32)
        t.l[...] = jnp.zeros(t.l.shape, F32)
        t.acc[...] = jnp.zeros(t.acc.shape, F32)

    def scores(t, k, stage, rows):
        s = jnp.dot(k, t.qt[...], preferred_element_type=F32)
        s_ref, c_ref = t.stage[stage]
        s_ref[0:rows, :] = s
        c_ref[...] = _stream_reduce(jnp.maximum, s)

    def scores_lat(j, stage):
        off = pl.multiple_of(j * ATTN_TK, ATTN_TK)
        for t in tiles:
            scores(t, k_ref[pl.ds(off, ATTN_TK), t.kv], stage, ATTN_TK)

    def scores_ctx():
        for t in tiles:
            scores(t, k_ref[n_lat:n_lat + n_ctx, t.kv], 0, n_ctx)

    def consume(t, vt, stage, rows):
        s_ref, c_ref = t.stage[stage]
        m_old = t.m[...]
        m_new = jnp.maximum(m_old, c_ref[...])
        alpha = jnp.exp2(m_old - m_new)
        p = jnp.exp2(s_ref[0:rows, :] - m_new).astype(BF16)
        vt_ones = jnp.concatenate([vt, jnp.ones((ONES_ROWS, rows), BF16)], axis=0)
        pv = jnp.dot(vt_ones, p, preferred_element_type=F32)
        t.l[...] = alpha * t.l[...] + pv[LANES:LANES + 1]
        t.acc[...] = alpha * t.acc[...] + pv[:LANES]
        t.m[...] = m_new

    def consume_lat(j, stage):
        for t in tiles:
            consume(t, vt_ref[j, t.kv, :], stage, ATTN_TK)

    if latent:
        scores_lat(0, 0)

        def body(i, carry):
            j = 2 * i
            scores_lat(j + 1, 1)
            consume_lat(j, 0)
            scores_lat(j + 2, 0)
            consume_lat(j + 1, 1)
            return carry
        lax.fori_loop(0, n_chunks // 2 - 1, body, 0)
        scores_lat(n_chunks - 1, 1)
        consume_lat(n_chunks - 2, 0)
        scores_ctx()
        consume_lat(n_chunks - 1, 1)
    else:
        scores_ctx()
    for t in tiles:
        consume(t, vt_ref[n_chunks, t.kv, 0:n_ctx], 0, n_ctx)


def _kv_specs(t, n_chunks, width, first_blk):
    return [pl.BlockSpec((t, width), lambda g, i: (0, first_blk + g)),
            pl.BlockSpec((n_chunks + 1, width, ATTN_TK), lambda g, i: (0, first_blk + g, 0))]


LATENT_ROW_BLOCKS = 2


def _row_block(rb):
    return slice(rb * ATTN_TQ, (rb + 1) * ATTN_TQ)


def _gqa_kernel(q_ref, k_ref, vt_ref, o_ref, *scratch, n_lat, n_ctx, latent, row_blocks):
    group = N_HEADS_A // N_KV_HEADS_A
    tiles = _make_tiles(scratch, [0] * (row_blocks * group))
    for rb in range(row_blocks):
        for h in range(group):
            tiles[rb * group + h].qt[...] = (
                q_ref[_row_block(rb), h * LANES:(h + 1) * LANES].astype(F32).T.astype(BF16))
    _flash_t(tiles, k_ref, vt_ref, n_lat=n_lat, n_ctx=n_ctx, latent=latent)
    for rb in range(row_blocks):
        for h in range(group):
            t = tiles[rb * group + h]
            o_ref[_row_block(rb), h * LANES:(h + 1) * LANES] = (t.acc[...] / t.l[...]).T.astype(o_ref.dtype)


def _attn_rows(n_lat, n_ctx, latent):
    if latent:
        assert n_lat % (LATENT_ROW_BLOCKS * ATTN_TQ) == 0
        return LATENT_ROW_BLOCKS, n_lat // (LATENT_ROW_BLOCKS * ATTN_TQ), 0
    assert n_ctx == ATTN_TQ
    return 1, 1, n_lat // ATTN_TQ


def _gqa_attention(q, k, vt, n_lat, n_ctx, latent):
    group = N_HEADS_A // N_KV_HEADS_A
    row_blocks, steps, first = _attn_rows(n_lat, n_ctx, latent)
    rows = row_blocks * ATTN_TQ
    return pl.pallas_call(
        functools.partial(_gqa_kernel, n_lat=n_lat, n_ctx=n_ctx, latent=latent, row_blocks=row_blocks),
        grid=(N_KV_HEADS_A, steps),
        in_specs=[pl.BlockSpec((rows, group * LANES), lambda g, i: (first + i, g))]
        + _kv_specs(n_lat + n_ctx, n_lat // ATTN_TK, LANES, 0),
        out_specs=pl.BlockSpec((rows, group * LANES), lambda g, i: (i, g)),
        out_shape=jax.ShapeDtypeStruct((steps * rows, QA_COLS), BF16),
        scratch_shapes=_attn_scratch(row_blocks * group),
        compiler_params=_params("arbitrary", "arbitrary"),
        name="gqa_attention" if latent else "gqa_attention_ctx",
    )(q, k, vt)


DIFF_HEADS_PER_STEP = 2


def _diff_kernel(q_ref, k_ref, vt_ref, lam_ref, gs_ref, o_ref, *scratch,
                 n_lat, n_ctx, lambda_init, latent, row_blocks):
    heads = DIFF_HEADS_PER_STEP
    tiles = _make_tiles(scratch, [h for _ in range(row_blocks) for h in range(heads) for _ in range(2)])
    lane = lax.broadcasted_iota(jnp.int32, (ATTN_TQ, LANES), 1)
    for rb in range(row_blocks):
        for h in range(heads):
            qh = q_ref[_row_block(rb), h * LANES:(h + 1) * LANES].astype(F32)
            zero = jnp.zeros_like(qh)
            first = (rb * heads + h) * 2
            tiles[first].qt[...] = jnp.where(lane < HEAD_DIM_B, qh, zero).T.astype(BF16)
            tiles[first + 1].qt[...] = jnp.where(lane >= HEAD_DIM_B, qh, zero).T.astype(BF16)
    _flash_t(tiles, k_ref, vt_ref, n_lat=n_lat, n_ctx=n_ctx, latent=latent)
    lv = lam_ref[...]
    lam = (jnp.exp(jnp.sum(lv[0:1] * lv[1:2], axis=-1, keepdims=True))
           - jnp.exp(jnp.sum(lv[2:3] * lv[3:4], axis=-1, keepdims=True)) + lambda_init)
    for rb in range(row_blocks):
        for h in range(heads):
            first = (rb * heads + h) * 2
            out1, out2 = (t.acc[...] / t.l[...] for t in tiles[first:first + 2])
            diff = (out1 - lam * out2).T
            o_ref[_row_block(rb), h * LANES:(h + 1) * LANES] = (
                _head_rmsnorm(diff, gs_ref[...]) * (1.0 - lambda_init)).astype(o_ref.dtype)


def _diff_attention(q, k, vt, lam_vecs, g_subln, lambda_init, n_lat, n_ctx, latent):
    width = DIFF_HEADS_PER_STEP * LANES
    assert QA_COLS % width == 0 and KA_COLS % width == 0
    row_blocks, steps, first = _attn_rows(n_lat, n_ctx, latent)
    rows = row_blocks * ATTN_TQ
    return pl.pallas_call(
        functools.partial(_diff_kernel, n_lat=n_lat, n_ctx=n_ctx, lambda_init=lambda_init,
                          latent=latent, row_blocks=row_blocks),
        grid=(N_HEADS_B // DIFF_HEADS_PER_STEP, steps),
        in_specs=[pl.BlockSpec((rows, width), lambda h, i: (first + i, QA_COLS // width + h))]
        + _kv_specs(n_lat + n_ctx, n_lat // ATTN_TK, width, KA_COLS // width)
        + [pl.BlockSpec((4, HEAD_DIM_B), lambda h, i: (0, 0)),
           pl.BlockSpec((1, LANES), lambda h, i: (0, 0))],
        out_specs=pl.BlockSpec((rows, width), lambda h, i: (i, h)),
        out_shape=jax.ShapeDtypeStruct((steps * rows, VB_COLS), BF16),
        scratch_shapes=_attn_scratch(row_blocks * DIFF_HEADS_PER_STEP * 2),
        compiler_params=_params("arbitrary", "arbitrary"),
        name="diff_attention" if latent else "diff_attention_ctx",
    )(q, k, vt, lam_vecs, g_subln.reshape(1, LANES))


def _dft_cos_sin(n):
    j = np.arange(n, dtype=np.float64)
    ang = 2.0 * np.pi * np.outer(j, j) / n
    return np.cos(ang), np.sin(ang)


def _channel_mix_weights(cs_ref, wf_ref, ab_ref):
    cs = cs_ref[...].astype(BF16)
    for g in range(N_FOURIER_GROUPS):
        ab_ref[g] = jnp.dot(cs, wf_ref[g].astype(BF16), preferred_element_type=F32).astype(BF16)


def _channel_mix(xr, xi, ab_ref, norm):
    outs = []
    for g in range(N_FOURIER_GROUPS):
        sl = slice(g * LANES, (g + 1) * LANES)
        lhs = jnp.concatenate([xr[:, sl], xi[:, sl]], axis=-1).astype(BF16)
        outs.append(jnp.dot(lhs, ab_ref[g], preferred_element_type=F32))
    return jnp.concatenate(outs, axis=-1) * norm


def _fourier_stage1_kernel(w_ref, u_ref, o_ref):
    o_ref[...] = jnp.dot(w_ref[...].astype(BF16), u_ref[...],
                         preferred_element_type=F32).astype(o_ref.dtype)


def _fourier_stage2_kernel(a_ref, tc_ref, ts_ref, w3_ref, cs_ref, wf_ref, o_ref, ab_ref, *, norm):
    @pl.when(pl.program_id(0) == 0)
    def _():
        _channel_mix_weights(cs_ref, wf_ref, ab_ref)

    ar = a_ref[0, 0].astype(F32)
    ai = a_ref[1, 0].astype(F32)
    tc, ts = tc_ref[0], ts_ref[0]
    br = ar * tc + ai * ts
    bi = ai * tc - ar * ts
    stack = jnp.concatenate([br, bi], axis=0).astype(BF16)
    x = jnp.dot(w3_ref[...].astype(BF16), stack, preferred_element_type=F32)
    half = x.shape[0] // 2
    o_ref[...] = _channel_mix(x[:half], x[half:], ab_ref, norm).astype(o_ref.dtype)


def _fourier_ctx_kernel(u_ref, w_ref, cs_ref, wf_ref, o_ref, ab_ref, *, norm):
    _channel_mix_weights(cs_ref, wf_ref, ab_ref)
    x = jnp.dot(w_ref[...].astype(BF16), u_ref[...], preferred_element_type=F32)
    half = x.shape[0] // 2
    o_ref[...] = _channel_mix(x[:half], x[half:], ab_ref, norm).astype(o_ref.dtype)


def _fourier_latent(u, n, w_fourier):
    assert u.shape[0] % LANES == 0
    n2 = LANES
    n1 = n // n2
    c1, s1 = _dft_cos_sin(n1)
    w1 = jnp.asarray(np.concatenate([c1, -s1], axis=0), F32)
    k1n2 = 2.0 * np.pi * np.outer(np.arange(n1), np.arange(n2)) / n
    tw_c = jnp.asarray(np.cos(k1n2)[:, :, None], F32)
    tw_s = jnp.asarray(np.sin(k1n2)[:, :, None], F32)
    c2, s2 = _dft_cos_sin(n2)
    w3 = jnp.asarray(np.block([[c2, s2], [-s2, c2]]), F32)
    cc, sc = _dft_cos_sin(FOURIER_GROUP_DIM)
    cs = jnp.asarray(np.concatenate([cc, sc], axis=0), F32)
    width = n2 * C_COLS
    cb = _largest_tile(width, (4096, 2048, 1024, 512))
    stage1 = pl.pallas_call(
        _fourier_stage1_kernel,
        grid=(width // cb,),
        in_specs=[pl.BlockSpec((2 * n1, n1), lambda j: (0, 0)),
                  pl.BlockSpec((n1, cb), lambda j: (0, j))],
        out_specs=pl.BlockSpec((2 * n1, cb), lambda j: (0, j)),
        out_shape=jax.ShapeDtypeStruct((2 * n1, width), BF16),
        compiler_params=_params("arbitrary"),
        name="fourier_stage1",
    )(w1, u.reshape(u.shape[0] // n2, width))
    a = stage1.reshape(2, n1, n2, C_COLS)
    norm = 1.0 / math.sqrt(n * FOURIER_GROUP_DIM)
    out = pl.pallas_call(
        functools.partial(_fourier_stage2_kernel, norm=norm),
        grid=(n1,),
        in_specs=[pl.BlockSpec((2, 1, n2, C_COLS), lambda k: (0, k, 0, 0)),
                  pl.BlockSpec((1, n2, 1), lambda k: (k, 0, 0)),
                  pl.BlockSpec((1, n2, 1), lambda k: (k, 0, 0)),
                  pl.BlockSpec((2 * n2, 2 * n2), lambda k: (0, 0)),
                  pl.BlockSpec((2 * FOURIER_GROUP_DIM, FOURIER_GROUP_DIM), lambda k: (0, 0)),
                  pl.BlockSpec((N_FOURIER_GROUPS, FOURIER_GROUP_DIM, FOURIER_GROUP_DIM),
                               lambda k: (0, 0, 0))],
        out_specs=pl.BlockSpec((n2, C_COLS), lambda k: (0, k)),
        out_shape=jax.ShapeDtypeStruct((n2, n1 * C_COLS), BF16),
        scratch_shapes=[pltpu.VMEM((N_FOURIER_GROUPS, 2 * FOURIER_GROUP_DIM, FOURIER_GROUP_DIM), BF16)],
        compiler_params=_params("arbitrary"),
        name="fourier_stage2",
    )(a, tw_c, tw_s, w3, cs, w_fourier)
    return out.reshape(n, C_COLS)


def _fourier_ctx(u_ctx, w_fourier):
    m = u_ctx.shape[0]
    c, s = _dft_cos_sin(m)
    w = jnp.asarray(np.concatenate([c, -s], axis=0), F32)
    cc, sc = _dft_cos_sin(FOURIER_GROUP_DIM)
    cs = jnp.asarray(np.concatenate([cc, sc], axis=0), F32)
    norm = 1.0 / math.sqrt(m * FOURIER_GROUP_DIM)
    return pl.pallas_call(
        functools.partial(_fourier_ctx_kernel, norm=norm),
        out_shape=jax.ShapeDtypeStruct((m, C_COLS), BF16),
        scratch_shapes=[pltpu.VMEM((N_FOURIER_GROUPS, 2 * FOURIER_GROUP_DIM, FOURIER_GROUP_DIM), BF16)],
        compiler_params=pltpu.CompilerParams(vmem_limit_bytes=VMEM_LIMIT),
        name="fourier_ctx",
    )(u_ctx, w, cs, w_fourier)


def _gated_residual(x, f, g, mod, gate_row):
    ms = jnp.mean(f * f, axis=-1, keepdims=True)
    return x + mod[gate_row:gate_row + 1] * (f * lax.rsqrt(ms + NORM_EPS) * g)


def _residual_kernel(x_ref, f_ref, g_ref, mod_ref, o_ref, *, gate_row):
    o_ref[...] = _gated_residual(x_ref[...], f_ref[...], g_ref[...], mod_ref[0], gate_row)


def _post_residual(x, f, g, mod, gate_row, n_rows, n_lat):
    d = x.shape[1]
    row = pl.BlockSpec((ROW_TILE, d), lambda i: (i, 0))
    return pl.pallas_call(
        functools.partial(_residual_kernel, gate_row=gate_row),
        grid=(n_rows // ROW_TILE,),
        in_specs=[row, row, pl.BlockSpec((1, d), lambda i: (0, 0)), _mod_spec(d, n_lat // ROW_TILE)],
        out_specs=row,
        out_shape=jax.ShapeDtypeStruct((n_rows, d), F32),
        compiler_params=_params("arbitrary"),
        name="post_residual",
    )(x, f, g.reshape(1, d), mod)


def _outproj_residual_kernel(*refs, n_pieces, gate_row):
    y_refs = refs[:n_pieces]
    w_ref, x_ref, g_ref, mod_ref, o_ref = refs[n_pieces:]
    y = jnp.concatenate([r[...] for r in y_refs], axis=-1)
    f = jnp.dot(y, w_ref[...], preferred_element_type=F32)
    o_ref[...] = _gated_residual(x_ref[...], f, g_ref[...], mod_ref[0], gate_row)


def _outproj_residual(y_pieces, w_bf16, x, g, mod, gate_row, n_rows, n_lat):
    d = x.shape[1]
    k = w_bf16.shape[0]
    assert sum(piece.shape[1] for piece in y_pieces) == k
    row = lambda w: pl.BlockSpec((ROW_TILE, w), lambda i: (i, 0))
    return pl.pallas_call(
        functools.partial(_outproj_residual_kernel, n_pieces=len(y_pieces), gate_row=gate_row),
        grid=(n_rows // ROW_TILE,),
        in_specs=[row(piece.shape[1]) for piece in y_pieces]
        + [pl.BlockSpec((k, d), lambda i: (0, 0)), row(d), pl.BlockSpec((1, d), lambda i: (0, 0)),
           _mod_spec(d, n_lat // ROW_TILE)],
        out_specs=row(d),
        out_shape=jax.ShapeDtypeStruct((n_rows, d), F32),
        compiler_params=_params("arbitrary"),
        name="out_proj_residual",
    )(*y_pieces, w_bf16, x, g.reshape(1, d), mod)


MOE_TM = 256
DISPATCH_TM = 2 * MOE_TM
R_IDX, R_W, R_RANK = 0, 2, 4


def _router_kernel(x_ref, g_ref, mod_ref, wr_ref, h_ref, route_ref, cnt_ref, carry_ref, *, shift_row):
    @pl.when(pl.program_id(0) == 0)
    def _():
        carry_ref[...] = jnp.zeros(carry_ref.shape, F32)

    h = _norm_modulate(x_ref[...], g_ref[...], mod_ref[0], shift_row)
    h_ref[...] = h
    logits = jnp.dot(h, wr_ref[...], preferred_element_type=F32, precision=lax.Precision.HIGHEST)
    rows = logits.shape[0]
    lane = lax.broadcasted_iota(jnp.int32, logits.shape, 1).astype(F32)
    neg = jnp.full(logits.shape, -jnp.inf, F32)
    l1 = jnp.where(lane < N_EXPERTS, logits, neg)
    m1 = jnp.max(l1, axis=-1, keepdims=True)
    i1 = jnp.min(jnp.where(l1 == m1, lane, float(LANES)), axis=-1, keepdims=True)
    l2 = jnp.where(lane == i1, neg, l1)
    m2 = jnp.max(l2, axis=-1, keepdims=True)
    i2 = jnp.min(jnp.where(l2 == m2, lane, float(LANES)), axis=-1, keepdims=True)
    e2 = jnp.exp(m2 - m1)
    w1 = 1.0 / (1.0 + e2)
    w2 = e2 / (1.0 + e2)
    hit1 = lane == i1
    hit2 = lane == i2
    onehot = jnp.logical_or(hit1, hit2).astype(BF16)
    r_i = lax.broadcasted_iota(jnp.int32, (rows, rows), 0)
    c_i = lax.broadcasted_iota(jnp.int32, (rows, rows), 1)
    before = (c_i < r_i).astype(BF16)
    base = carry_ref[...] + jnp.dot(before, onehot, preferred_element_type=F32)
    rank1 = jnp.sum(jnp.where(hit1, base, 0.0), axis=-1, keepdims=True)
    rank2 = jnp.sum(jnp.where(hit2, base, 0.0), axis=-1, keepdims=True)
    carry_ref[...] = carry_ref[...] + jnp.sum(onehot.astype(F32), axis=0, keepdims=True)
    rec = jnp.zeros(logits.shape, F32)
    for off, val in ((R_IDX, i1), (R_IDX + 1, i2), (R_W, w1), (R_W + 1, w2),
                     (R_RANK, rank1), (R_RANK + 1, rank2)):
        rec = jnp.where(lane == off, val, rec)
    route_ref[...] = rec
    cnt_ref[...] = jnp.broadcast_to(carry_ref[...], cnt_ref.shape)


def _router(x, g, mod, w_router, n_lat):
    d = x.shape[1]
    wr = jnp.zeros((d, LANES), F32).at[:, :N_EXPERTS].set(w_router)
    return pl.pallas_call(
        functools.partial(_router_kernel, shift_row=3),
        grid=(n_lat // ROW_TILE,),
        in_specs=[pl.BlockSpec((ROW_TILE, d), lambda i: (i, 0)),
                  pl.BlockSpec((1, d), lambda i: (0, 0)),
                  pl.BlockSpec((1, N_MOD, d), lambda i: (0, 0, 0)),
                  pl.BlockSpec((d, LANES), lambda i: (0, 0))],
        out_specs=[pl.BlockSpec((ROW_TILE, d), lambda i: (i, 0)),
                   pl.BlockSpec((ROW_TILE, LANES), lambda i: (i, 0)),
                   pl.BlockSpec((8, LANES), lambda i: (0, 0))],
        out_shape=[jax.ShapeDtypeStruct((n_lat, d), F32),
                   jax.ShapeDtypeStruct((n_lat, LANES), F32),
                   jax.ShapeDtypeStruct((8, LANES), F32)],
        scratch_shapes=[pltpu.VMEM((1, LANES), F32)],
        compiler_params=_params("arbitrary"),
        name="router",
    )(x, g.reshape(1, d), mod, wr)


def _row_copy(src_hbm, src_row, dst_ref, dst_row, sem):
    return pltpu.make_async_copy(src_hbm.at[pl.ds(src_row, 1)], dst_ref.at[pl.ds(dst_row, 1)], sem)


def _wait_rows(src_hbm, dst_ref, sem):
    pltpu.make_async_copy(src_hbm.at[pl.ds(0, dst_ref.shape[0])], dst_ref, sem).wait()


ISSUE_UNROLL = 8


def _dispatch_kernel(src_ref, nv_ref, h_hbm, o_ref, buf_ref, sems):
    i = pl.program_id(0)
    n_valid = nv_ref[0]
    slot = lax.rem(i, 2)

    def issue(tile, slot):
        base = tile * DISPATCH_TM

        def body(r, carry):
            _row_copy(h_hbm, src_ref[base + r], buf_ref.at[slot], r, sems.at[slot]).start()
            return carry
        lax.fori_loop(0, DISPATCH_TM, body, 0, unroll=ISSUE_UNROLL)

    @pl.when(jnp.logical_and(i == 0, n_valid > 0))
    def _():
        issue(0, 0)

    @pl.when(i + 1 < n_valid)
    def _():
        issue(i + 1, 1 - slot)

    @pl.when(i < n_valid)
    def _():
        _wait_rows(h_hbm, buf_ref.at[slot], sems.at[slot])
        o_ref[...] = buf_ref[slot].astype(o_ref.dtype)

    @pl.when(i >= n_valid)
    def _():
        o_ref[...] = jnp.zeros(o_ref.shape, o_ref.dtype)


def _dispatch(src_token, n_valid, h):
    d = h.shape[1]
    n_slots = src_token.shape[0]
    grid_spec = pltpu.PrefetchScalarGridSpec(
        num_scalar_prefetch=2,
        grid=(n_slots // DISPATCH_TM,),
        in_specs=[pl.BlockSpec(memory_space=pl.ANY)],
        out_specs=pl.BlockSpec((DISPATCH_TM, d), lambda i, src, nv: (i, 0)),
        scratch_shapes=[pltpu.VMEM((2, DISPATCH_TM, d), F32), pltpu.SemaphoreType.DMA((2,))],
    )
    return pl.pallas_call(
        _dispatch_kernel,
        grid_spec=grid_spec,
        out_shape=jax.ShapeDtypeStruct((n_slots, d), BF16),
        compiler_params=_params("arbitrary"),
        name="moe_dispatch",
    )(src_token, n_valid, h)


def _combine_kernel(dest_ref, y_hbm, route_ref, x_ref, g_ref, mod_ref, o_ref, buf_ref, sems, *, gate_row):
    i = pl.program_id(0)
    slot = lax.rem(i, 2)

    def issue(tile, slot):
        base = tile * ROW_TILE

        def body(r, carry):
            for k in range(TOP_K):
                _row_copy(y_hbm, dest_ref[TOP_K * (base + r) + k], buf_ref.at[slot], k * ROW_TILE + r,
                          sems.at[slot]).start()
            return carry
        lax.fori_loop(0, ROW_TILE, body, 0, unroll=ISSUE_UNROLL)

    @pl.when(i == 0)
    def _():
        issue(0, 0)

    @pl.when(i + 1 < pl.num_programs(0))
    def _():
        issue(i + 1, 1 - slot)

    _wait_rows(y_hbm, buf_ref.at[slot], sems.at[slot])
    route = route_ref[...]
    f = (route[:, R_W:R_W + 1] * buf_ref[slot, 0:ROW_TILE, :]
         + route[:, R_W + 1:R_W + 2] * buf_ref[slot, ROW_TILE:TOP_K * ROW_TILE, :])
    o_ref[...] = _gated_residual(x_ref[...], f, g_ref[...], mod_ref[0], gate_row)


def _combine_residual(dest_flat, y_sorted, route, x, g, mod, gate_row, n_lat):
    d = x.shape[1]
    row = lambda w: pl.BlockSpec((ROW_TILE, w), lambda i, dest: (i, 0))
    grid_spec = pltpu.PrefetchScalarGridSpec(
        num_scalar_prefetch=1,
        grid=(n_lat // ROW_TILE,),
        in_specs=[pl.BlockSpec(memory_space=pl.ANY), row(LANES), row(d),
                  pl.BlockSpec((1, d), lambda i, dest: (0, 0)),
                  pl.BlockSpec((1, N_MOD, d), lambda i, dest: (0, 0, 0))],
        out_specs=row(d),
        scratch_shapes=[pltpu.VMEM((2, TOP_K * ROW_TILE, d), F32), pltpu.SemaphoreType.DMA((2,))],
    )
    return pl.pallas_call(
        functools.partial(_combine_kernel, gate_row=gate_row),
        grid_spec=grid_spec,
        out_shape=jax.ShapeDtypeStruct((n_lat, d), F32),
        compiler_params=_params("arbitrary"),
        name="moe_combine",
    )(dest_flat, y_sorted, route, x, g.reshape(1, d), mod)


def _tile_meta(starts, counts, ends, n_slots, tm):
    tile_start = jnp.arange(n_slots // tm, dtype=jnp.int32) * tm
    group = jnp.minimum(jnp.sum((tile_start[:, None] >= ends[None, :]).astype(jnp.int32), axis=1),
                        N_EXPERTS - 1)
    rows = jnp.clip(starts[group] + counts[group] - tile_start, 0, tm)
    return group, rows


def _moe_layer(x, g_pre, g_post, mod, w_router, w_gate, w_up, w_down, n_lat):
    d_ff = w_gate.shape[-1]
    h, route, cnt = _router(x, g_pre, mod, w_router, n_lat)
    experts = route[:, R_IDX:R_IDX + TOP_K].astype(jnp.int32)
    ranks = route[:, R_RANK:R_RANK + TOP_K].astype(jnp.int32)
    counts = cnt[0, :N_EXPERTS].astype(jnp.int32)
    padded = ((counts + MOE_TM - 1) // MOE_TM) * MOE_TM
    ends = jnp.cumsum(padded)
    starts = ends - padded
    dest = (starts[experts] + ranks).reshape(-1)
    n_slots = TOP_K * n_lat + N_EXPERTS * MOE_TM
    assert n_slots % DISPATCH_TM == 0
    n_valid = ((ends[-1:] + DISPATCH_TM - 1) // DISPATCH_TM).astype(jnp.int32)

    token = jnp.repeat(jnp.arange(n_lat, dtype=jnp.int32), TOP_K)
    src_token = jnp.zeros((n_slots,), jnp.int32).at[dest].set(token)
    hs = _dispatch(src_token, n_valid, h)
    group, rows = _tile_meta(starts, counts, ends, n_slots, MOE_TM)
    act = _gmm(hs, [w_gate, w_up], group, rows, tm=MOE_TM,
               tn=_largest_tile(d_ff, (1024, 512, 256, 128)), out_dtype=BF16, swiglu=True, name="moe_gate_up")
    y = _gmm(act, [w_down], group, rows, tm=MOE_TM,
             tn=_largest_tile(x.shape[1], (512, 256, 128)), out_dtype=F32, name="moe_down")
    return _combine_residual(dest, y, route, x, g_post, mod, 5, n_lat)


def kernel(x, c, ctx, c_ctx, w_mod, b_mod, g_pre_mix, g_post_mix, g_pre_ffn, g_post_ffn, w_in, g_q_a, g_k_a, lambda_q1, lambda_k1, lambda_q2, lambda_k2, g_subln_b, w_fourier, w_out, w_gate_dense, w_up_dense, w_down_dense, w_router, w_gate_moe, w_up_moe, w_down_moe):
    batch, n_lat, d = x.shape
    n_ctx = ctx.shape[1]
    depth = w_mod.shape[0]
    assert batch == 1 and n_ctx == ROW_TILE and n_lat % max(2 * ATTN_TK, MOE_TM, GRID_W) == 0
    assert depth == 2, "layer 0 dense, layer 1 (last) routed experts"
    t = n_lat + n_ctx

    cond_rows = jnp.concatenate([c, c_ctx[None, :], jnp.zeros((6, d), F32)], axis=0)
    mod_all = _modulation(cond_rows, w_mod, b_mod)
    rope_a = _rope_tables(n_lat, n_ctx, HEAD_DIM_A, 1)
    rope_b = _rope_tables(n_lat, n_ctx, HEAD_DIM_B, 2)

    xs = jnp.concatenate([x[0], ctx[0]], axis=0)
    for layer in range(depth):
        last = layer == depth - 1
        n_out = n_lat if last else t
        mod = mod_all[layer, :2].reshape(2, N_MOD, d)
        lambda_init = 0.8 - 0.6 * math.exp(-0.3 * layer)

        q, k, u, vt = _mixer_front(xs, g_pre_mix[layer], mod, w_in[layer].astype(BF16),
                                   g_q_a[layer], g_k_a[layer], rope_a, rope_b, n_lat)
        lam_vecs = jnp.stack([lambda_q1[layer], lambda_k1[layer], lambda_q2[layer], lambda_k2[layer]])
        mixers = [[_gqa_attention(q, k, vt, n_lat, n_ctx, latent),
                   _diff_attention(q, k, vt, lam_vecs, g_subln_b[layer], lambda_init, n_lat, n_ctx, latent),
                   _fourier_latent(u, n_lat, w_fourier[layer]) if latent
                   else _fourier_ctx(u[n_lat:], w_fourier[layer])]
                  for latent in ((True,) if last else (True, False))]
        y_a, y_b, y_c = (jnp.concatenate(parts, axis=0) for parts in zip(*mixers))
        xs = _outproj_residual([y_a, y_b, y_c], w_out[layer].astype(BF16), xs, g_post_mix[layer], mod, 2,
                               n_out, n_lat)

        j = layer // 2
        if layer % 2 == 0:
            h = _prenorm(xs, g_pre_ffn[layer], mod, 3, n_out, n_lat)
            act = _dense_matmul(h, w_gate_dense[j], w2=w_up_dense[j], swiglu=True, tn=512,
                                out_dtype=BF16, name="ffn_gate_up")
            f = _dense_matmul(act, w_down_dense[j], tn=512, tiles=(ROW_TILE,), out_dtype=F32, name="ffn_down")
            xs = _post_residual(xs, f, g_post_ffn[layer], mod, 5, n_out, n_lat)
        else:
            xs = _moe_layer(xs, g_pre_ffn[layer], g_post_ffn[layer], mod, w_router[j],
                            w_gate_moe[j], w_up_moe[j], w_down_moe[j], n_lat)
    return xs[:n_lat][None]
```
